```python
import jax, jax.numpy as jnp
from jax import lax
import numpy as np

D_MODEL = 2048
BATCH = 4
SEQ = 4096
DEPTH = 4

N_A = DEPTH // 2
N_B = DEPTH - N_A
RET_DIM = 256
RET_HEADS = D_MODEL // RET_DIM
RET_WIDTH = RET_HEADS * RET_DIM
RET_CHUNK = 128
RET_THETA = 10000.0
SWA_DIM = 64
SWA_HEADS = D_MODEL // SWA_DIM
SWA_KV_HEADS = SWA_HEADS // 8
SWA_WIDTH = SWA_HEADS * SWA_DIM
SWA_KV_WIDTH = SWA_KV_HEADS * SWA_DIM
WINDOW = 128
ROPE_THETA = 500000.0
ROT_DIM = SWA_DIM // 4
MEM_TOKENS = 256
MEM_HEADS = 4
MEM_DIM = D_MODEL // 8
MEM_WIDTH = MEM_HEADS * MEM_DIM
IN_A = 4 * RET_WIDTH + 2 * MEM_WIDTH
IN_B = 2 * SWA_WIDTH + 2 * MEM_WIDTH
MIX_WIDTH = RET_WIDTH + MEM_WIDTH
EPS = 1e-6

kernel_name = "yoco_retention_swa_sinks_memxattn"


def rms_norm(x, g):
    xf = x.astype(jnp.float32)
    y = xf * lax.rsqrt(jnp.mean(xf * xf, axis=-1, keepdims=True) + EPS)
    return (y * g.astype(jnp.float32)).astype(x.dtype)


def head_rms(x):
    xf = x.astype(jnp.float32)
    return xf * lax.rsqrt(jnp.mean(xf * xf, axis=-1, keepdims=True) + EPS)


def rope(x, positions, rot_dim, theta):
    inv = theta ** (-jnp.arange(0, rot_dim, 2, dtype=jnp.float32) / rot_dim)
    ang = positions.astype(jnp.float32)[..., None] * inv
    cos = jnp.cos(ang)[:, :, None, :].astype(x.dtype)
    sin = jnp.sin(ang)[:, :, None, :].astype(x.dtype)
    half = rot_dim // 2
    x1, x2, rest = x[..., :half], x[..., half:rot_dim], x[..., rot_dim:]
    return jnp.concatenate([x1 * cos - x2 * sin, x2 * cos + x1 * sin, rest], axis=-1)


def retention_chunkwise(q, k, v):
    B, S, H, dk = q.shape
    dv = v.shape[-1]
    C = RET_CHUNK
    N = S // C
    f32 = jnp.float32
    log_gamma = jnp.log(1.0 - jnp.exp2(-5.0 - jnp.arange(H, dtype=f32)))

    def to_chunks(t):
        return t.astype(f32).reshape(B, N, C, H, t.shape[-1]).transpose(1, 0, 3, 2, 4)

    qc, kc, vc = to_chunks(q), to_chunks(k), to_chunks(v)
    idx = jnp.arange(C, dtype=f32)
    diff = idx[:, None] - idx[None, :]
    decay = jnp.where(diff[None] >= 0,
                      jnp.exp(log_gamma[:, None, None] * jnp.maximum(diff, 0.0)[None]), 0.0)
    xi = jnp.exp(log_gamma[:, None] * (idx + 1.0))[:, :, None]
    zeta = jnp.exp(log_gamma[:, None] * (C - 1.0 - idx))[:, :, None]
    g_chunk = jnp.exp(log_gamma * C)[:, None, None]

    def step(R, inp):
        q_, k_, v_ = inp
        inner = jnp.einsum('bhnd,bhmd->bhnm', q_, k_) * decay
        o = (jnp.einsum('bhnm,bhmv->bhnv', inner, v_)
             + jnp.einsum('bhnd,bhdv->bhnv', q_ * xi, R))
        R = R * g_chunk + jnp.einsum('bhmd,bhmv->bhdv', k_ * zeta, v_)
        return R, o

    R0 = jnp.zeros((B, H, dk, dv), f32)
    _, o = lax.scan(step, R0, (qc, kc, vc))
    return o.transpose(1, 0, 3, 2, 4).reshape(B, S, H, dv)


def swa_sinks(q, k, v, sinks):
    B, S, Hq, d = q.shape
    Hkv = k.shape[2]
    G = Hq // Hkv
    W = WINDOW
    N = S // W
    qb = q.reshape(B, N, W, Hkv, G, d)
    kb = k.reshape(B, N, W, Hkv, d)
    vb = v.reshape(B, N, W, Hkv, d)
    zeros = jnp.zeros_like(kb[:, :1])
    kk = jnp.concatenate([jnp.concatenate([zeros, kb[:, :-1]], axis=1), kb], axis=2)
    vv = jnp.concatenate([jnp.concatenate([zeros, vb[:, :-1]], axis=1), vb], axis=2)
    s = jnp.einsum('bnqkgd,bnjkd->bnkgqj', qb, kk).astype(jnp.float32) * (d ** -0.5)
    qi = jnp.arange(W)[:, None] + W
    kj = jnp.arange(2 * W)[None, :]
    rel = qi - kj
    band = (rel >= 0) & (rel < W)
    not_pad = (jnp.arange(N)[:, None, None] > 0) | (kj[None] >= W)
    mask = band[None] & not_pad
    s = jnp.where(mask[None, :, None, None], s, -1e30)
    sink = sinks.astype(jnp.float32).reshape(1, 1, Hkv, G, 1, 1)
    m = jnp.maximum(jnp.max(s, axis=-1, keepdims=True), sink)
    p = jnp.exp(s - m)
    denom = jnp.sum(p, axis=-1, keepdims=True) + jnp.exp(sink - m)
    p = (p / denom).astype(v.dtype)
    o = jnp.einsum('bnkgqj,bnjkd->bnqkgd', p, vv)
    return o.reshape(B, S, Hq * d)


def mem_attention(qm, mk, mv):
    s = jnp.einsum('bshd,bmhd->bhsm', qm, mk).astype(jnp.float32) * (qm.shape[-1] ** -0.5)
    p = jax.nn.softmax(s, axis=-1).astype(mv.dtype)
    o = jnp.einsum('bhsm,bmhd->bshd', p, mv)
    return o.reshape(qm.shape[0], qm.shape[1], -1)


def setup_inputs(seed: int = 0) -> dict:
    key = jax.random.key(seed)
    ks = jax.random.split(key, 14)
    f32 = jnp.float32

    def w(k, shape, fan_in):
        return jax.random.normal(k, shape, f32) * (fan_in ** -0.5)

    x = jax.random.normal(ks[0], (BATCH, SEQ, D_MODEL), f32)
    mem = jax.random.normal(ks[1], (BATCH, MEM_TOKENS, D_MODEL), f32)
    positions = (jax.random.randint(ks[2], (BATCH, 1), 0, 1024, dtype=jnp.int32)
                 + jnp.arange(SEQ, dtype=jnp.int32)[None, :])
    pre_norm_g = 1.0 + 0.05 * jax.random.normal(ks[3], (DEPTH, D_MODEL), f32)
    post_norm_g = 1.0 + 0.05 * jax.random.normal(ks[4], (DEPTH, D_MODEL), f32)
    mem_norm_g = 1.0 + 0.05 * jax.random.normal(ks[5], (DEPTH, D_MODEL), f32)
    kv_norm_g = 1.0 + 0.05 * jax.random.normal(ks[6], (D_MODEL,), f32)
    w_in_a = w(ks[7], (N_A, D_MODEL, IN_A), D_MODEL)
    w_in_b = w(ks[8], (N_B, D_MODEL, IN_B), D_MODEL)
    w_kv_b = w(ks[9], (D_MODEL, 2 * SWA_KV_WIDTH), D_MODEL)
    sinks_b = 0.5 * jax.random.normal(ks[10], (N_B, SWA_HEADS), f32)
    w_mem_kv = w(ks[11], (DEPTH, D_MODEL, 2 * MEM_WIDTH), D_MODEL)
    w_out = w(ks[12], (DEPTH, MIX_WIDTH, D_MODEL), MIX_WIDTH)
    return {"x": x, "mem": mem, "positions": positions,
            "pre_norm_g": pre_norm_g, "post_norm_g": post_norm_g,
            "mem_norm_g": mem_norm_g, "kv_norm_g": kv_norm_g,
            "w_in_a": w_in_a, "w_in_b": w_in_b, "w_kv_b": w_kv_b,
            "sinks_b": sinks_b, "w_mem_kv": w_mem_kv, "w_out": w_out}


def reference(x, mem, positions, pre_norm_g, post_norm_g, mem_norm_g, kv_norm_g,
              w_in_a, w_in_b, w_kv_b, sinks_b, w_mem_kv, w_out):
    B, S, _ = x.shape
    M = mem.shape[1]
    h = x
    k_shared = None
    v_shared = None
    for l in range(DEPTH):
        y = rms_norm(h, pre_norm_g[l])
        mkv = rms_norm(mem, mem_norm_g[l]) @ w_mem_kv[l]
        mk = mkv[..., :MEM_WIDTH].reshape(B, M, MEM_HEADS, MEM_DIM)
        mv = mkv[..., MEM_WIDTH:].reshape(B, M, MEM_HEADS, MEM_DIM)
        if l < N_A:
            z = y @ w_in_a[l]
            q, k, v, g, qm, gm = jnp.split(
                z, np.cumsum([RET_WIDTH] * 4 + [MEM_WIDTH]).tolist(), axis=-1)
            q = rope(q.reshape(B, S, RET_HEADS, RET_DIM), positions, RET_DIM, RET_THETA)
            k = rope(k.reshape(B, S, RET_HEADS, RET_DIM), positions, RET_DIM, RET_THETA) * (RET_DIM ** -0.5)
            v = v.reshape(B, S, RET_HEADS, RET_DIM)
            o_main = head_rms(retention_chunkwise(q, k, v)).astype(x.dtype).reshape(B, S, RET_WIDTH)
        else:
            z = y @ w_in_b[l - N_A]
            q, g, qm, gm = jnp.split(
                z, np.cumsum([SWA_WIDTH, SWA_WIDTH, MEM_WIDTH]).tolist(), axis=-1)
            q = rope(q.reshape(B, S, SWA_HEADS, SWA_DIM), positions, ROT_DIM, ROPE_THETA)
            o_main = swa_sinks(q, k_shared, v_shared, sinks_b[l - N_A])
        o_main = o_main * jax.nn.silu(g)
        o_mem = mem_attention(qm.reshape(B, S, MEM_HEADS, MEM_DIM), mk, mv) * jax.nn.silu(gm)
        out = jnp.concatenate([o_main, o_mem], axis=-1) @ w_out[l]
        h = h + rms_norm(out, post_norm_g[l])
        if l == N_A - 1:
            kv = rms_norm(h, kv_norm_g) @ w_kv_b
            k_shared = rope(kv[..., :SWA_KV_WIDTH].reshape(B, S, SWA_KV_HEADS, SWA_DIM),
                            positions, ROT_DIM, ROPE_THETA)
            v_shared = kv[..., SWA_KV_WIDTH:].reshape(B, S, SWA_KV_HEADS, SWA_DIM)
    return h
```

```python
import functools

import jax
import jax.numpy as jnp
import numpy as np
from jax import lax
from jax.experimental import pallas as pl
from jax.experimental.pallas import tpu as pltpu

F32 = jnp.float32
BF16 = jnp.bfloat16

D_MODEL = 2048
DEPTH = 4
N_A = DEPTH // 2
RET_DIM = 256
RET_HEADS = D_MODEL // RET_DIM
RET_WIDTH = RET_HEADS * RET_DIM
RET_CHUNK = 128
RET_THETA = 10000.0
SWA_DIM = 64
SWA_HEADS = D_MODEL // SWA_DIM
SWA_KV_HEADS = SWA_HEADS // 8
SWA_WIDTH = SWA_HEADS * SWA_DIM
SWA_KV_WIDTH = SWA_KV_HEADS * SWA_DIM
WINDOW = 128
ROPE_THETA = 500000.0
ROT_DIM = SWA_DIM // 4
MEM_HEADS = 4
MEM_DIM = D_MODEL // 8
MEM_WIDTH = MEM_HEADS * MEM_DIM
EPS = 1e-6
MASK_VALUE = -1e30

LANES = 128
VMEM_LIMIT = 56 * 1024 * 1024

NT_DIMS = (((1,), (1,)), ((), ()))
TN_DIMS = (((0,), (0,)), ((), ()))


def _params(n_axes):
    return pltpu.CompilerParams(dimension_semantics=("arbitrary",) * n_axes,
                                vmem_limit_bytes=VMEM_LIMIT)


def _silu(g):
    return g * (1.0 / (1.0 + jnp.exp(-g)))


def _rms(x):
    return x * lax.rsqrt(jnp.mean(x * x, axis=-1, keepdims=True) + EPS)


def _tables_kernel(pos_ref, inv_r_ref, inv_s_ref, sgn_ref, cr_ref, sr_ref, cs_ref, ss_ref):
    pos = pos_ref[...]
    ang_r = pos * inv_r_ref[...]
    cr_ref[...] = jnp.cos(ang_r)
    sr_ref[...] = jnp.sin(ang_r)
    ang_s = pos * inv_s_ref[...]
    cs_ref[...] = jnp.cos(ang_s)
    ss_ref[...] = jnp.sin(ang_s) * sgn_ref[...]


def _rotary_tables(pos, tm=1024):
    M = pos.shape[0]
    tm = min(tm, M)
    inv_r = RET_THETA ** (-jnp.arange(0, RET_DIM, 2, dtype=F32) / RET_DIM)
    inv_h = ROPE_THETA ** (-jnp.arange(0, ROT_DIM, 2, dtype=F32) / ROT_DIM)
    half = ROT_DIM // 2
    head = jnp.concatenate([inv_h, inv_h, jnp.zeros((SWA_DIM - ROT_DIM,), F32)])
    sgn_h = jnp.concatenate([-jnp.ones((half,), F32), jnp.ones((half,), F32),
                             jnp.zeros((SWA_DIM - ROT_DIM,), F32)])
    inv_s = jnp.tile(head, LANES // SWA_DIM)
    sgn = jnp.tile(sgn_h, LANES // SWA_DIM)
    row = pl.BlockSpec((1, LANES), lambda i: (0, 0))
    tab = pl.BlockSpec((tm, LANES), lambda i: (i, 0))
    shape = jax.ShapeDtypeStruct((M, LANES), F32)
    return pl.pallas_call(
        _tables_kernel, grid=(M // tm,),
        in_specs=[pl.BlockSpec((tm, 1), lambda i: (i, 0)), row, row, row],
        out_specs=[tab] * 4, out_shape=[shape] * 4,
        compiler_params=_params(1), name="rotary_tables",
    )(pos, inv_r[None], inv_s[None], sgn[None])


def _norm_kernel(x_ref, g_ref, o_ref):
    o_ref[...] = (_rms(x_ref[...]) * g_ref[...]).astype(o_ref.dtype)


def _rmsnorm_multi(x, gains, tm=512):
    M, D = x.shape
    G = gains.shape[0]
    tm = min(tm, M)
    return pl.pallas_call(
        _norm_kernel, grid=(M // tm, G),
        in_specs=[pl.BlockSpec((tm, D), lambda i, g: (i, 0)),
                  pl.BlockSpec((None, 1, D), lambda i, g: (g, 0, 0))],
        out_specs=pl.BlockSpec((None, tm, D), lambda i, g: (g, i, 0)),
        out_shape=jax.ShapeDtypeStruct((G, M, D), BF16),
        compiler_params=_params(2), name="rmsnorm",
    )(x, gains[:, None, :])


def _matmul_kernel(a_ref, w_ref, o_ref):
    o_ref[...] = jnp.dot(a_ref[...], w_ref[...], preferred_element_type=F32).astype(o_ref.dtype)


def _matmul(a, w, layer, tm=1024, tn=1024):
    M, K = a.shape
    N = w.shape[-1]
    tm, tn = min(tm, M), min(tn, N)
    return pl.pallas_call(
        _matmul_kernel, grid=(N // tn, M // tm),
        in_specs=[pl.BlockSpec((tm, K), lambda j, i: (i, 0)),
                  pl.BlockSpec((None, K, tn), lambda j, i: (layer, 0, j))],
        out_specs=pl.BlockSpec((tm, tn), lambda j, i: (i, j)),
        out_shape=jax.ShapeDtypeStruct((M, N), BF16),
        compiler_params=_params(2), name="matmul",
    )(a, w)


def _retention_kernel(q_ref, k_ref, v_ref, g_ref, cos_ref, sin_ref, o_ref,
                      r_ref, dec_ref, xi_ref, zeta_ref, gc_ref, *, n_chunks):
    C = RET_CHUNK
    h = pl.program_id(1)
    half = RET_DIM // 2

    @pl.when(pl.program_id(2) == 0)
    def _init():
        r_ref[...] = jnp.zeros_like(r_ref)
        def log_gamma(shape):
            return jnp.log(1.0 - jnp.exp2(-5.0 - jnp.full(shape, h, jnp.int32).astype(F32)))

        n = lax.broadcasted_iota(jnp.int32, (C, RET_DIM), 0).astype(F32)
        xi_ref[...] = jnp.exp(log_gamma((C, RET_DIM)) * (n + 1.0))
        zeta_ref[...] = jnp.exp(log_gamma((C, RET_DIM)) * (C - 1.0 - n))
        gc_ref[...] = jnp.exp(log_gamma(gc_ref.shape) * float(C))
        diff = (lax.broadcasted_iota(jnp.int32, (C, C), 0)
                - lax.broadcasted_iota(jnp.int32, (C, C), 1)).astype(F32)
        dec_ref[...] = jnp.where(diff >= 0, jnp.exp(log_gamma((C, C)) * jnp.maximum(diff, 0.0)), 0.0)

    def rope(x, cos, sin):
        x1, x2 = x[:, :half], x[:, half:]
        return jnp.concatenate([x1 * cos - x2 * sin, x2 * cos + x1 * sin], axis=-1)

    for c in range(n_chunks):
        rows = pl.ds(c * C, C)
        cos, sin = cos_ref[rows, :], sin_ref[rows, :]
        q = rope(q_ref[rows, :].astype(F32), cos, sin).astype(BF16)
        k = rope(k_ref[rows, :].astype(F32), cos, sin) * (RET_DIM ** -0.5)
        kz = (k * zeta_ref[...]).astype(BF16)
        k = k.astype(BF16)
        v = v_ref[rows, :]
        r = r_ref[...]
        inner = lax.dot_general(q, k, NT_DIMS, preferred_element_type=F32) * dec_ref[...]
        o = (jnp.dot(inner.astype(BF16), v, preferred_element_type=F32)
             + xi_ref[...] * jnp.dot(q, r.astype(BF16), preferred_element_type=F32))
        r_ref[...] = r * gc_ref[0:1, :] + lax.dot_general(kz, v, TN_DIMS, preferred_element_type=F32)
        o_ref[rows, :] = (_rms(o) * _silu(g_ref[rows, :].astype(F32))).astype(o_ref.dtype)


def _retention(z, cos, sin, batch, ts=512):
    M = z.shape[0]
    S = M // batch
    ts = min(ts, S)
    nt = S // ts
    H = RET_HEADS

    def col(off):
        return pl.BlockSpec((ts, RET_DIM), lambda b, h, t: (b * nt + t, off + h))

    tab = pl.BlockSpec((ts, LANES), lambda b, h, t: (b * nt + t, 0))
    return pl.pallas_call(
        functools.partial(_retention_kernel, n_chunks=ts // RET_CHUNK),
        grid=(batch, H, nt),
        in_specs=[col(0), col(H), col(2 * H), col(3 * H), tab, tab],
        out_specs=pl.BlockSpec((ts, RET_DIM), lambda b, h, t: (b * nt + t, h)),
        out_shape=jax.ShapeDtypeStruct((M, RET_WIDTH), BF16),
        scratch_shapes=[pltpu.VMEM((RET_DIM, RET_DIM), F32),
                        pltpu.VMEM((RET_CHUNK, RET_CHUNK), F32),
                        pltpu.VMEM((RET_CHUNK, RET_DIM), F32),
                        pltpu.VMEM((RET_CHUNK, RET_DIM), F32),
                        pltpu.VMEM((8, RET_DIM), F32)],
        compiler_params=_params(3), name="retention",
    )(z, z, z, z, cos, sin)


def _mem_attn_kernel(q_ref, g_ref, k_ref, v_ref, o_ref):
    s = lax.dot_general(q_ref[...], k_ref[...], NT_DIMS, preferred_element_type=F32) * (MEM_DIM ** -0.5)
    e = jnp.exp(s - jnp.max(s, axis=-1, keepdims=True))
    p = (e / jnp.sum(e, axis=-1, keepdims=True)).astype(BF16)
    o = jnp.dot(p, v_ref[...], preferred_element_type=F32)
    o_ref[...] = (o * _silu(g_ref[...].astype(F32))).astype(o_ref.dtype)


def _mem_attention(z, q_off, mkv, batch, ts=512):
    M = z.shape[0]
    S = M // batch
    Mm = mkv.shape[0] // batch
    ts = min(ts, S)
    nt = S // ts
    H = MEM_HEADS
    return pl.pallas_call(
        _mem_attn_kernel, grid=(batch, nt, H),
        in_specs=[pl.BlockSpec((ts, MEM_DIM), lambda b, t, h: (b * nt + t, q_off + h)),
                  pl.BlockSpec((ts, MEM_DIM), lambda b, t, h: (b * nt + t, q_off + H + h)),
                  pl.BlockSpec((Mm, MEM_DIM), lambda b, t, h: (b, h)),
                  pl.BlockSpec((Mm, MEM_DIM), lambda b, t, h: (b, H + h))],
        out_specs=pl.BlockSpec((ts, MEM_DIM), lambda b, t, h: (b * nt + t, h)),
        out_shape=jax.ShapeDtypeStruct((M, MEM_WIDTH), BF16),
        compiler_params=_params(3), name="mem_attention",
    )(z, z, mkv, mkv)


def _swa_rope(x, cos, sin_signed):
    lane = lax.broadcasted_iota(jnp.int32, x.shape, 1)
    first_half = (lane & (SWA_DIM - 1)) < (ROT_DIM // 2)
    partner = jnp.where(first_half, pltpu.roll(x, LANES - ROT_DIM // 2, 1), pltpu.roll(x, ROT_DIM // 2, 1))
    return x * cos + partner * sin_signed


def _dup_heads(x):
    low = lax.broadcasted_iota(jnp.int32, x.shape, 1) < SWA_DIM
    swapped = pltpu.roll(x, SWA_DIM, 1)
    return jnp.where(low, x, swapped), jnp.where(low, swapped, x)


def _kv_kernel(y_ref, w_ref, cos_ref, sin_ref, k_ref, v_ref):
    kv = jnp.dot(y_ref[...], w_ref[...], preferred_element_type=F32)
    cos, sin = cos_ref[...], sin_ref[...]
    for blk in range(SWA_KV_WIDTH // LANES):
        k = _swa_rope(kv[:, blk * LANES:(blk + 1) * LANES], cos, sin)
        v = kv[:, SWA_KV_WIDTH + blk * LANES:SWA_KV_WIDTH + (blk + 1) * LANES]
        for dst_ref, pair in ((k_ref, _dup_heads(k)), (v_ref, _dup_heads(v))):
            for i, t in enumerate(pair):
                dst_ref[:, (2 * blk + i) * LANES:(2 * blk + i + 1) * LANES] = t.astype(dst_ref.dtype)


def _shared_kv(y, w, cos, sin, tm=512):
    M, D = y.shape
    tm = min(tm, M)
    tab = pl.BlockSpec((tm, LANES), lambda i: (i, 0))
    out = pl.BlockSpec((tm, 2 * SWA_KV_WIDTH), lambda i: (i, 0))
    shape = jax.ShapeDtypeStruct((M, 2 * SWA_KV_WIDTH), BF16)
    return pl.pallas_call(
        _kv_kernel, grid=(M // tm,),
        in_specs=[pl.BlockSpec((tm, D), lambda i: (i, 0)),
                  pl.BlockSpec((D, 2 * SWA_KV_WIDTH), lambda i: (0, 0)), tab, tab],
        out_specs=[out, out], out_shape=[shape, shape],
        compiler_params=_params(1), name="shared_kv",
    )(y, w, cos, sin)


def _swa_kernel(sink_ref, q_ref, g_ref, kp_ref, kc_ref, vp_ref, vc_ref, cos_ref, sin_ref, o_ref):
    W = WINDOW
    n = pl.program_id(1)
    cos, sin = cos_ref[...], sin_ref[...]
    low = lax.broadcasted_iota(jnp.int32, (W, LANES), 1) < SWA_DIM
    row = lax.broadcasted_iota(jnp.int32, (W, W), 0)
    col = lax.broadcasted_iota(jnp.int32, (W, W), 1)
    mask = jnp.concatenate([(col > row) & (n > 0), col <= row], axis=1)
    mask = jnp.concatenate([mask, mask], axis=0)
    top = lax.broadcasted_iota(jnp.int32, (2 * W, 1), 0) < W
    pairs_per_kv = SWA_HEADS // SWA_KV_HEADS // 2
    for kh in range(SWA_KV_HEADS):
        kcols = slice(kh * LANES, (kh + 1) * LANES)
        k2 = jnp.concatenate([kp_ref[:, kcols], kc_ref[:, kcols]], axis=0)
        v2 = jnp.concatenate([vp_ref[:, kcols], vc_ref[:, kcols]], axis=0)
        for j in range(pairs_per_kv):
            pair = kh * pairs_per_kv + j
            cols = slice(pair * LANES, (pair + 1) * LANES)
            q = _swa_rope(q_ref[:, cols].astype(F32), cos, sin) * (SWA_DIM ** -0.5)
            q2 = jnp.concatenate([jnp.where(low, q, 0.0), jnp.where(low, 0.0, q)], axis=0).astype(BF16)
            s = lax.dot_general(q2, k2, NT_DIMS, preferred_element_type=F32)
            s = jnp.where(mask, s, MASK_VALUE)
            sink = jnp.where(top, sink_ref[2 * pair], sink_ref[2 * pair + 1])
            m = jnp.maximum(jnp.max(s, axis=-1, keepdims=True), sink)
            e = jnp.exp(s - m)
            denom = jnp.sum(e, axis=-1, keepdims=True) + jnp.exp(sink - m)
            o2 = jnp.dot((e / denom).astype(BF16), v2, preferred_element_type=F32)
            o = jnp.where(low, o2[:W], o2[W:])
            o_ref[:, cols] = (o * _silu(g_ref[:, cols].astype(F32))).astype(o_ref.dtype)


def _swa(z, k2, v2, sinks, cos, sin, batch):
    M = z.shape[0]
    nb = M // batch // WINDOW

    def cur(width, off=0):
        return pl.BlockSpec((WINDOW, width), lambda b, n: (b * nb + n, off))

    prev = pl.BlockSpec((WINDOW, 2 * SWA_KV_WIDTH), lambda b, n: (b * nb + jnp.maximum(n - 1, 0), 0))
    return pl.pallas_call(
        _swa_kernel, grid=(batch, nb),
        in_specs=[pl.BlockSpec(memory_space=pltpu.SMEM),
                  cur(SWA_WIDTH), cur(SWA_WIDTH, 1),
                  prev, cur(2 * SWA_KV_WIDTH), prev, cur(2 * SWA_KV_WIDTH),
                  cur(LANES), cur(LANES)],
        out_specs=cur(SWA_WIDTH),
        out_shape=jax.ShapeDtypeStruct((M, SWA_WIDTH), BF16),
        compiler_params=_params(2), name="swa",
    )(sinks, z, z, k2, k2, v2, v2, cos, sin)


def _out_kernel(a1_ref, a2_ref, w1_ref, w2_ref, h_ref, pg_ref, *rest, n_next):
    ng_ref = rest[0] if n_next else None
    h_out_ref = rest[1] if n_next else rest[0]
    acc = (jnp.dot(a1_ref[...], w1_ref[...], preferred_element_type=F32)
           + jnp.dot(a2_ref[...], w2_ref[...], preferred_element_type=F32))
    h = h_ref[...] + _rms(acc) * pg_ref[...]
    h_out_ref[...] = h
    if n_next:
        y_ref = rest[2]
        hn = _rms(h)
        for g in range(n_next):
            y_ref[g] = (hn * ng_ref[g]).astype(y_ref.dtype)


def _out_proj(o_main, o_mem, w_out, layer, h, post_g, next_g, tm=256):
    M = h.shape[0]
    tm = min(tm, M)
    n_next = 0 if next_g is None else next_g.shape[0]
    k1, k2 = o_main.shape[1], o_mem.shape[1]
    assert k1 % k2 == 0
    row = lambda width: pl.BlockSpec((tm, width), lambda i: (i, 0))
    in_specs = [row(k1), row(k2),
                pl.BlockSpec((None, k1, D_MODEL), lambda i: (layer, 0, 0)),
                pl.BlockSpec((None, k2, D_MODEL), lambda i: (layer, k1 // k2, 0)),
                row(D_MODEL),
                pl.BlockSpec((1, D_MODEL), lambda i: (0, 0))]
    args = [o_main, o_mem, w_out, w_out, h, post_g[None]]
    out_specs = [row(D_MODEL)]
    out_shape = [jax.ShapeDtypeStruct((M, D_MODEL), F32)]
    if n_next:
        in_specs.append(pl.BlockSpec((n_next, 1, D_MODEL), lambda i: (0, 0, 0)))
        args.append(next_g[:, None, :])
        out_specs.append(pl.BlockSpec((n_next, tm, D_MODEL), lambda i: (0, i, 0)))
        out_shape.append(jax.ShapeDtypeStruct((n_next, M, D_MODEL), BF16))
    res = pl.pallas_call(
        functools.partial(_out_kernel, n_next=n_next), grid=(M // tm,),
        in_specs=in_specs, out_specs=out_specs, out_shape=out_shape,
        compiler_params=_params(1), name="out_proj",
    )(*args)
    return (res[0], res[1]) if n_next else (res[0], None)


def kernel(x, mem, positions, pre_norm_g, post_norm_g, mem_norm_g, kv_norm_g,
           w_in_a, w_in_b, w_kv_b, sinks_b, w_mem_kv, w_out):
    B, S, D = x.shape
    Mm = mem.shape[1]
    M = B * S
    w_in_a, w_in_b, w_kv_b, w_mem_kv, w_out = (
        w.astype(BF16) for w in (w_in_a, w_in_b, w_kv_b, w_mem_kv, w_out))

    cos_r, sin_r, cos_s, sin_s = _rotary_tables(positions.astype(F32).reshape(M, 1))
    mem_n = _rmsnorm_multi(mem.reshape(B * Mm, D), mem_norm_g)
    h = x.reshape(M, D)
    y = _rmsnorm_multi(h, pre_norm_g[:1])[0]
    k2 = v2 = None
    for l in range(DEPTH):
        mkv = _matmul(mem_n[l], w_mem_kv, l)
        if l < N_A:
            z = _matmul(y, w_in_a, l)
            o_main = _retention(z, cos_r, sin_r, B)
            o_mem = _mem_attention(z, 4 * RET_HEADS, mkv, B)
        else:
            z = _matmul(y, w_in_b, l - N_A)
            o_main = _swa(z, k2, v2, sinks_b[l - N_A], cos_s, sin_s, B)
            o_mem = _mem_attention(z, 2 * SWA_WIDTH // MEM_DIM, mkv, B)
        if l == N_A - 1:
            next_g = jnp.stack([pre_norm_g[l + 1], kv_norm_g])
        elif l < DEPTH - 1:
            next_g = pre_norm_g[l + 1][None]
        else:
            next_g = None
        h, ys = _out_proj(o_main, o_mem, w_out, l, h, post_norm_g[l], next_g)
        if ys is not None:
            y = ys[0]
        if l == N_A - 1:
            k2, v2 = _shared_kv(ys[1], w_kv_b, cos_s, sin_s)
    return h.reshape(B, S, D)
```

```python
import functools

import jax
import jax.numpy as jnp
from jax import lax
from jax.experimental import pallas as pl
from jax.experimental.pallas import tpu as pltpu

F32 = jnp.float32
BF16 = jnp.bfloat16

D_MODEL = 2048
DEPTH = 4
N_A = DEPTH // 2
RET_DIM = 256
RET_HEADS = D_MODEL // RET_DIM
RET_WIDTH = RET_HEADS * RET_DIM
RET_CHUNK = 128
RET_THETA = 10000.0
SWA_DIM = 64
SWA_HEADS = D_MODEL // SWA_DIM
SWA_KV_HEADS = SWA_HEADS // 8
SWA_WIDTH = SWA_HEADS * SWA_DIM
SWA_KV_WIDTH = SWA_KV_HEADS * SWA_DIM
WINDOW = 128
ROPE_THETA = 500000.0
ROT_DIM = SWA_DIM // 4
MEM_HEADS = 4
MEM_DIM = D_MODEL // 8
MEM_WIDTH = MEM_HEADS * MEM_DIM
EPS = 1e-6
MASK_VALUE = -1e30

LANES = 128
VMEM_LIMIT = 56 * 1024 * 1024

NT_DIMS = (((1,), (1,)), ((), ()))
TN_DIMS = (((0,), (0,)), ((), ()))

IN_TILE = 1024
PLAIN, SILU, ROPE_RET, ROPE_SWA = range(4)
TILES_A = ((ROPE_RET, (0, 1, 2, 3)), (PLAIN, (4, 5, 8)), (SILU, (6, 7, 9)))
TILES_B = ((ROPE_SWA, (0, 1)), (PLAIN, (4,)), (SILU, (2, 3, 5)))


def _params(n_axes):
    return pltpu.CompilerParams(dimension_semantics=("arbitrary",) * n_axes,
                                vmem_limit_bytes=VMEM_LIMIT)


def _silu(g):
    return g * (1.0 / (1.0 + jnp.exp(-g)))


def _rms(x):
    return x * lax.rsqrt(jnp.mean(x * x, axis=-1, keepdims=True) + EPS)


def _tables_kernel(pos_ref, inv_r_ref, inv_s_ref, sgn_ref, cr_ref, sr_ref, ck_ref, sk_ref, cq_ref, sq_ref):
    pos = pos_ref[...]
    ang_r = pos * inv_r_ref[...]
    cr_ref[...] = jnp.cos(ang_r)
    sr_ref[...] = jnp.sin(ang_r)
    ang_s = pos * inv_s_ref[...]
    cos_s = jnp.cos(ang_s)
    sin_s = jnp.sin(ang_s) * sgn_ref[...]
    ck_ref[...] = cos_s
    sk_ref[...] = sin_s
    cq_ref[...] = cos_s * (SWA_DIM ** -0.5)
    sq_ref[...] = sin_s * (SWA_DIM ** -0.5)


def _rotary_tables(pos, tm=1024):
    M = pos.shape[0]
    tm = min(tm, M)
    inv_r = RET_THETA ** (-jnp.arange(0, RET_DIM, 2, dtype=F32) / RET_DIM)
    inv_h = ROPE_THETA ** (-jnp.arange(0, ROT_DIM, 2, dtype=F32) / ROT_DIM)
    half = ROT_DIM // 2
    head = jnp.concatenate([inv_h, inv_h, jnp.zeros((SWA_DIM - ROT_DIM,), F32)])
    sgn_h = jnp.concatenate([-jnp.ones((half,), F32), jnp.ones((half,), F32),
                             jnp.zeros((SWA_DIM - ROT_DIM,), F32)])
    inv_s = jnp.tile(head, LANES // SWA_DIM)
    sgn = jnp.tile(sgn_h, LANES // SWA_DIM)
    row = pl.BlockSpec((1, LANES), lambda i: (0, 0))
    tab = pl.BlockSpec((tm, LANES), lambda i: (i, 0))
    shape = jax.ShapeDtypeStruct((M, LANES), F32)
    return pl.pallas_call(
        _tables_kernel, grid=(M // tm,),
        in_specs=[pl.BlockSpec((tm, 1), lambda i: (i, 0)), row, row, row],
        out_specs=[tab] * 6, out_shape=[shape] * 6,
        compiler_params=_params(1), name="rotary_tables",
    )(pos, inv_r[None], inv_s[None], sgn[None])


def _norm_kernel(x_ref, g_ref, o_ref):
    o_ref[...] = (_rms(x_ref[...]) * g_ref[...]).astype(o_ref.dtype)


def _rmsnorm_multi(x, gains, tm=512):
    M, D = x.shape
    G = gains.shape[0]
    tm = min(tm, M)
    return pl.pallas_call(
        _norm_kernel, grid=(M // tm, G),
        in_specs=[pl.BlockSpec((tm, D), lambda i, g: (i, 0)),
                  pl.BlockSpec((None, 1, D), lambda i, g: (g, 0, 0))],
        out_specs=pl.BlockSpec((None, tm, D), lambda i, g: (g, i, 0)),
        out_shape=jax.ShapeDtypeStruct((G, M, D), BF16),
        compiler_params=_params(2), name="rmsnorm",
    )(x, gains[:, None, :])


def _mem_kv_kernel(a_ref, w_ref, o_ref):
    acc = jnp.dot(a_ref[...], w_ref[...].astype(BF16), preferred_element_type=F32)
    scale = jnp.where(pl.program_id(1) * 2 < pl.num_programs(1), MEM_DIM ** -0.5, 1.0)
    o_ref[...] = (acc * scale).astype(o_ref.dtype)


def _mem_kv(mem_n, w, tn=512):
    L, R, D = mem_n.shape
    N = w.shape[-1]
    return pl.pallas_call(
        _mem_kv_kernel, grid=(L, N // tn),
        in_specs=[pl.BlockSpec((None, R, D), lambda l, j: (l, 0, 0)),
                  pl.BlockSpec((None, D, tn), lambda l, j: (l, 0, j))],
        out_specs=pl.BlockSpec((None, R, tn), lambda l, j: (l, 0, j)),
        out_shape=jax.ShapeDtypeStruct((L, R, N), BF16),
        compiler_params=_params(2), name="mem_kv",
    )(mem_n, w)


def _swa_rope(x, cos, sin_signed):
    lane = lax.broadcasted_iota(jnp.int32, x.shape, 1)
    first_half = (lane & (SWA_DIM - 1)) < (ROT_DIM // 2)
    partner = jnp.where(first_half, pltpu.roll(x, LANES - ROT_DIM // 2, 1), pltpu.roll(x, ROT_DIM // 2, 1))
    return x * cos + partner * sin_signed


def _in_proj_kernel(a_ref, w_ref, *rest, kind):
    o_ref, wb_ref = rest[-2:]

    @pl.when(pl.program_id(1) == 0)
    def _cast_weight():
        wb_ref[...] = w_ref[...].astype(BF16)

    acc = jnp.dot(a_ref[...], wb_ref[...], preferred_element_type=F32)
    if kind == PLAIN:
        o_ref[...] = acc.astype(o_ref.dtype)
    elif kind == SILU:
        o_ref[...] = _silu(acc).astype(o_ref.dtype)
    elif kind == ROPE_RET:
        cos, sin = rest[0][...], rest[1][...]
        half = RET_DIM // 2
        for c in range(0, IN_TILE, RET_DIM):
            x1, x2 = acc[:, c:c + half], acc[:, c + half:c + RET_DIM]
            o_ref[:, c:c + half] = (x1 * cos - x2 * sin).astype(o_ref.dtype)
            o_ref[:, c + half:c + RET_DIM] = (x2 * cos + x1 * sin).astype(o_ref.dtype)
    else:
        cos, sin = rest[0][...], rest[1][...]
        for c in range(0, IN_TILE, LANES):
            o_ref[:, c:c + LANES] = _swa_rope(acc[:, c:c + LANES], cos, sin).astype(o_ref.dtype)


def _in_proj(a, w, layer, kind, tiles, cos=None, sin=None, tm=1024):
    M, K = a.shape
    tm = min(tm, M)

    def tile_of(j):
        t = tiles[-1]
        for jj in range(len(tiles) - 1):
            t = jnp.where(j == jj, tiles[jj], t)
        return t

    in_specs = [pl.BlockSpec((tm, K), lambda j, i: (i, 0)),
                pl.BlockSpec((None, K, IN_TILE), lambda j, i: (layer, 0, tile_of(j)))]
    args = [a, w]
    if kind in (ROPE_RET, ROPE_SWA):
        in_specs += [pl.BlockSpec((tm, LANES), lambda j, i: (i, 0))] * 2
        args += [cos, sin]
    return pl.pallas_call(
        functools.partial(_in_proj_kernel, kind=kind), grid=(len(tiles), M // tm),
        in_specs=in_specs,
        out_specs=pl.BlockSpec((tm, IN_TILE), lambda j, i: (i, j)),
        out_shape=jax.ShapeDtypeStruct((M, len(tiles) * IN_TILE), BF16),
        scratch_shapes=[pltpu.VMEM((K, IN_TILE), BF16)],
        compiler_params=_params(2), name="in_proj",
    )(*args)


def _retention_kernel(q_ref, k_ref, v_ref, sg_ref, o_ref, r_ref, dec_ref, xi_ref, zeta_ref, gc_ref, *, n_chunks):
    C = RET_CHUNK
    h = pl.program_id(1)
    k_scale = RET_DIM ** -0.5

    @pl.when(pl.program_id(2) == 0)
    def _init():
        def log_gamma(shape):
            return jnp.log(1.0 - jnp.exp2(-5.0 - jnp.full(shape, h, jnp.int32).astype(F32)))

        r_ref[...] = jnp.zeros_like(r_ref)
        n = lax.broadcasted_iota(jnp.int32, (C, RET_DIM), 0).astype(F32)
        xi_ref[...] = jnp.exp(log_gamma((C, RET_DIM)) * (n + 1.0))
        zeta_ref[...] = jnp.exp(log_gamma((C, RET_DIM)) * (C - 1.0 - n)) * k_scale
        gc_ref[...] = jnp.exp(log_gamma(gc_ref.shape) * float(C))
        diff = (lax.broadcasted_iota(jnp.int32, (C, C), 0)
                - lax.broadcasted_iota(jnp.int32, (C, C), 1)).astype(F32)
        dec_ref[...] = jnp.where(diff >= 0, jnp.exp(log_gamma((C, C)) * jnp.maximum(diff, 0.0)) * k_scale, 0.0)

    def qk(c):
        rows = pl.ds(c * C, C)
        return lax.dot_general(q_ref[rows, :], k_ref[rows, :], NT_DIMS, preferred_element_type=F32)

    r = r_ref[...]
    inner_next = qk(0)
    for c in range(n_chunks):
        rows = pl.ds(c * C, C)
        inner = inner_next
        if c + 1 < n_chunks:
            inner_next = qk(c + 1)
        q, v = q_ref[rows, :], v_ref[rows, :]
        kz = (k_ref[rows, :].astype(F32) * zeta_ref[...]).astype(BF16)
        update = lax.dot_general(kz, v, TN_DIMS, preferred_element_type=F32)
        cross = jnp.dot(q, r.astype(BF16), preferred_element_type=F32)
        o = jnp.dot((inner * dec_ref[...]).astype(BF16), v, preferred_element_type=F32) + xi_ref[...] * cross
        r = r * gc_ref[0:1, :] + update
        o_ref[rows, :] = (_rms(o) * sg_ref[rows, :].astype(F32)).astype(o_ref.dtype)
    r_ref[...] = r


def _retention(z_qk, z_v, z_sg, batch, ts=2048):
    M = z_qk.shape[0]
    S = M // batch
    ts = min(ts, S)
    nt = S // ts
    H = RET_HEADS

    def col(off):
        return pl.BlockSpec((ts, RET_DIM), lambda b, h, t: (b * nt + t, off + h))

    return pl.pallas_call(
        functools.partial(_retention_kernel, n_chunks=ts // RET_CHUNK),
        grid=(batch, H, nt),
        in_specs=[col(0), col(H), col(0), col(0)],
        out_specs=pl.BlockSpec((ts, RET_DIM), lambda b, h, t: (b * nt + t, h)),
        out_shape=jax.ShapeDtypeStruct((M, RET_WIDTH), BF16),
        scratch_shapes=[pltpu.VMEM((RET_DIM, RET_DIM), F32),
                        pltpu.VMEM((RET_CHUNK, RET_CHUNK), F32),
                        pltpu.VMEM((RET_CHUNK, RET_DIM), F32),
                        pltpu.VMEM((RET_CHUNK, RET_DIM), F32),
                        pltpu.VMEM((8, RET_DIM), F32)],
        compiler_params=_params(3), name="retention",
    )(z_qk, z_qk, z_v, z_sg)


def _kv_kernel(y_ref, w_ref, cos_ref, sin_ref, k_ref, vt_ref, wb_ref):
    @pl.when(pl.program_id(0) == 0)
    def _cast_weight():
        wb_ref[...] = w_ref[...].astype(BF16)

    kv = jnp.dot(y_ref[...], wb_ref[...], preferred_element_type=F32)
    cos, sin = cos_ref[...], sin_ref[...]
    low = lax.broadcasted_iota(jnp.int32, (kv.shape[0], LANES), 1) < SWA_DIM
    for blk in range(SWA_KV_WIDTH // LANES):
        k = _swa_rope(kv[:, blk * LANES:(blk + 1) * LANES], cos, sin)
        swapped = pltpu.roll(k, SWA_DIM, 1)
        tiles = (jnp.where(low, k, 0.0), jnp.where(low, 0.0, swapped),
                 jnp.where(low, swapped, 0.0), jnp.where(low, 0.0, k))
        for i, t in enumerate(tiles):
            k_ref[:, (4 * blk + i) * LANES:(4 * blk + i + 1) * LANES] = t.astype(k_ref.dtype)
    vt_ref[...] = kv[:, SWA_KV_WIDTH:].T.astype(vt_ref.dtype)


SWA_K_COLS = 2 * LANES * SWA_KV_HEADS


def _shared_kv(y, w, cos, sin, tm=512):
    M, D = y.shape
    tm = min(tm, M)
    tab = pl.BlockSpec((tm, LANES), lambda i: (i, 0))
    return pl.pallas_call(
        _kv_kernel, grid=(M // tm,),
        in_specs=[pl.BlockSpec((tm, D), lambda i: (i, 0)),
                  pl.BlockSpec((D, 2 * SWA_KV_WIDTH), lambda i: (0, 0)), tab, tab],
        out_specs=[pl.BlockSpec((tm, SWA_K_COLS), lambda i: (i, 0)),
                   pl.BlockSpec((SWA_KV_WIDTH, tm), lambda i: (0, i))],
        out_shape=[jax.ShapeDtypeStruct((M, SWA_K_COLS), BF16),
                   jax.ShapeDtypeStruct((SWA_KV_WIDTH, M), BF16)],
        scratch_shapes=[pltpu.VMEM((D, 2 * SWA_KV_WIDTH), BF16)],
        compiler_params=_params(1), name="shared_kv",
    )(y, w, cos, sin)


def _swa_kernel(sink_ref, q_ref, sg_ref, kp_ref, kc_ref, vtp_ref, vtc_ref, o_ref):
    W = WINDOW
    n_pairs = SWA_HEADS // SWA_KV_HEADS // 2
    QN = n_pairs * W
    jrow = lax.broadcasted_iota(jnp.int32, (W, QN), 0)
    icol = lax.broadcasted_iota(jnp.int32, (W, QN), 1)
    upper = jrow > (icol & (W - 1))
    pair_of_lane = lax.broadcasted_iota(jnp.int32, (1, QN), 1) // W

    def body(has_prev):
        nk = 2 * W if has_prev else W

        def scores(kh):
            key_tiles = []
            for parity in range(2):
                kcols = slice((2 * kh + parity) * LANES, (2 * kh + parity + 1) * LANES)
                key_tiles += ([kp_ref[:, kcols]] if has_prev else []) + [kc_ref[:, kcols]]
            keys = jnp.concatenate(key_tiles, axis=0)
            q = jnp.concatenate([q_ref[:, (kh * n_pairs + j) * LANES:(kh * n_pairs + j + 1) * LANES]
                                 for j in range(n_pairs)], axis=0)
            return lax.dot_general(keys, q, NT_DIMS, preferred_element_type=F32)

        def attend(kh, st):
            vrows = slice(kh * SWA_DIM, (kh + 1) * SWA_DIM)
            vt = vtc_ref[vrows, :]
            if has_prev:
                vt = jnp.concatenate([vtp_ref[vrows, :], vt], axis=1)
            pairs = [kh * n_pairs + j for j in range(n_pairs)]
            ot = []
            for parity in range(2):
                sp = st[parity * nk:(parity + 1) * nk]
                s = jnp.where(upper, sp[:W], sp[W:]) if has_prev else jnp.where(upper, MASK_VALUE, sp)
                sink = jnp.full((1, QN), sink_ref[2 * pairs[-1] + parity], F32)
                for j in range(n_pairs - 1):
                    sink = jnp.where(pair_of_lane == j, sink_ref[2 * pairs[j] + parity], sink)
                m = jnp.maximum(jnp.max(s, axis=0, keepdims=True), sink)
                e = jnp.exp(s - m)
                denom = jnp.sum(e, axis=0, keepdims=True) + jnp.exp(sink - m)
                if has_prev:
                    e = jnp.concatenate([jnp.where(upper, e, 0.0), jnp.where(upper, 0.0, e)], axis=0)
                ot.append(jnp.dot(vt, e.astype(BF16), preferred_element_type=F32) / denom)
            for j, p in enumerate(pairs):
                cols = slice(p * LANES, (p + 1) * LANES)
                o = jnp.concatenate([ot[0][:, j * W:(j + 1) * W], ot[1][:, j * W:(j + 1) * W]], axis=0).T
                o_ref[:, cols] = (o * sg_ref[:, cols].astype(F32)).astype(o_ref.dtype)

        st = scores(0)
        for kh in range(SWA_KV_HEADS):
            st_next = scores(kh + 1) if kh + 1 < SWA_KV_HEADS else None
            attend(kh, st)
            st = st_next

    n = pl.program_id(1)
    pl.when(n == 0)(functools.partial(body, False))
    pl.when(n > 0)(functools.partial(body, True))


def _swa(z_q, z_sg, k2, vt, sinks, batch):
    M = z_q.shape[0]
    nb = M // batch // WINDOW

    def rows(width, off=0, back=0):
        return pl.BlockSpec((WINDOW, width), lambda b, n: (b * nb + jnp.maximum(n - back, 0), off))

    def vcols(back):
        return pl.BlockSpec((SWA_KV_WIDTH, WINDOW), lambda b, n: (0, b * nb + jnp.maximum(n - back, 0)))

    return pl.pallas_call(
        _swa_kernel, grid=(batch, nb),
        in_specs=[pl.BlockSpec(memory_space=pltpu.SMEM),
                  rows(SWA_WIDTH), rows(SWA_WIDTH),
                  rows(SWA_K_COLS, back=1), rows(SWA_K_COLS), vcols(1), vcols(0)],
        out_specs=rows(SWA_WIDTH),
        out_shape=jax.ShapeDtypeStruct((M, SWA_WIDTH), BF16),
        compiler_params=_params(2), name="swa",
    )(sinks, z_q, z_sg, k2, k2, vt, vt)


OUT_CHUNK = 128


def _out_kernel(a_ref, qm_ref, sgm_ref, mk_ref, mv_ref, w1_ref, w2_ref, h_ref, pg_ref, *rest, n_next):
    ng_ref = rest[0] if n_next else None
    h_out_ref = rest[1] if n_next else rest[0]
    y_refs = rest[2:2 + n_next]
    om_ref = rest[-1]
    tm = a_ref.shape[0]
    heads = [slice(hh * MEM_DIM, (hh + 1) * MEM_DIM) for hh in range(MEM_HEADS)]
    chunks = [pl.ds(r, OUT_CHUNK) for r in range(0, tm, OUT_CHUNK)]

    scores = [lax.dot_general(qm_ref[:, cols], mk_ref[:, cols], NT_DIMS, preferred_element_type=F32)
              for cols in heads]
    acc = jnp.dot(a_ref[chunks[0], :], w1_ref[...], preferred_element_type=F32)
    for cols, s in zip(heads, scores):
        e = jnp.exp(s - jnp.max(s, axis=-1, keepdims=True))
        p = (e / jnp.sum(e, axis=-1, keepdims=True)).astype(BF16)
        o = jnp.dot(p, mv_ref[:, cols], preferred_element_type=F32)
        om_ref[:, cols] = (o * sgm_ref[:, cols].astype(F32)).astype(om_ref.dtype)
    for c, rows in enumerate(chunks):
        if c:
            acc = jnp.dot(a_ref[rows, :], w1_ref[...], preferred_element_type=F32)
        acc = acc + jnp.dot(om_ref[rows, :], w2_ref[...], preferred_element_type=F32)
        h = h_ref[rows, :] + _rms(acc) * pg_ref[...]
        h_out_ref[rows, :] = h
        if n_next:
            hn = _rms(h)
            for g in range(n_next):
                y_refs[g][rows, :] = (hn * ng_ref[g]).astype(y_refs[g].dtype)


def _mix_out(o_main, z_qm, z_sgm, mkv, w_out, layer, h, post_g, next_g, batch, tm=512):
    M = h.shape[0]
    tm = min(tm, M)
    tiles_per_batch = M // batch // tm
    Mm = mkv.shape[1] // batch
    n_next = len(next_g)
    k1 = o_main.shape[1]
    assert k1 % MEM_WIDTH == 0

    def row(width, off=0):
        return pl.BlockSpec((tm, width), lambda i: (i, off))

    def mem(off):
        return pl.BlockSpec((None, Mm, MEM_WIDTH), lambda i: (layer, i // tiles_per_batch, off))

    once = pl.Buffered(1)
    in_specs = [row(k1), row(MEM_WIDTH, z_qm.shape[1] // MEM_WIDTH - 1),
                row(MEM_WIDTH, z_sgm.shape[1] // MEM_WIDTH - 1), mem(0), mem(1),
                pl.BlockSpec((None, k1, D_MODEL), lambda i: (layer, 0, 0), pipeline_mode=once),
                pl.BlockSpec((None, MEM_WIDTH, D_MODEL), lambda i: (layer, k1 // MEM_WIDTH, 0), pipeline_mode=once),
                row(D_MODEL),
                pl.BlockSpec((1, D_MODEL), lambda i: (0, 0))]
    args = [o_main, z_qm, z_sgm, mkv, mkv, w_out, w_out, h, post_g[None]]
    if n_next:
        in_specs.append(pl.BlockSpec((n_next, 1, D_MODEL), lambda i: (0, 0, 0)))
        args.append(jnp.stack(next_g)[:, None, :])
    out_specs = [row(D_MODEL)] * (1 + n_next)
    out_shape = [jax.ShapeDtypeStruct((M, D_MODEL), F32)] + [jax.ShapeDtypeStruct((M, D_MODEL), BF16)] * n_next
    res = pl.pallas_call(
        functools.partial(_out_kernel, n_next=n_next), grid=(M // tm,),
        in_specs=in_specs, out_specs=out_specs, out_shape=out_shape,
        scratch_shapes=[pltpu.VMEM((tm, MEM_WIDTH), BF16)],
        compiler_params=_params(1), name="mix_out",
    )(*args)
    return res[0], res[1:]


def kernel(x, mem, positions, pre_norm_g, post_norm_g, mem_norm_g, kv_norm_g,
           w_in_a, w_in_b, w_kv_b, sinks_b, w_mem_kv, w_out):
    B, S, D = x.shape
    Mm = mem.shape[1]
    M = B * S
    w_out = w_out.astype(BF16)

    cos_r, sin_r, cos_k, sin_k, cos_q, sin_q = _rotary_tables(positions.astype(F32).reshape(M, 1))
    mkv = _mem_kv(_rmsnorm_multi(mem.reshape(B * Mm, D), mem_norm_g), w_mem_kv)
    h = x.reshape(M, D)
    y = _rmsnorm_multi(h, pre_norm_g[:1]).reshape(M, D)
    k2 = vt = None
    for l in range(DEPTH):
        if l < N_A:
            z_rope, z_plain, z_silu = (_in_proj(y, w_in_a, l, kind, tiles, cos_r, sin_r) for kind, tiles in TILES_A)
            o_main = _retention(z_rope, z_plain, z_silu, B)
        else:
            z_rope, z_plain, z_silu = (_in_proj(y, w_in_b, l - N_A, kind, tiles, cos_q, sin_q)
                                       for kind, tiles in TILES_B)
            o_main = _swa(z_rope, z_silu, k2, vt, sinks_b[l - N_A], B)
        next_g = [pre_norm_g[l + 1]] if l < DEPTH - 1 else []
        if l == N_A - 1:
            next_g.append(kv_norm_g)
        h, ys = _mix_out(o_main, z_plain, z_silu, mkv, w_out, l, h, post_norm_g[l], next_g, B)
        if ys:
            y = ys[0]
        if l == N_A - 1:
            k2, vt = _shared_kv(ys[1], w_kv_b, cos_k, sin_k)
    return h.reshape(B, S, D)
```

```python
import functools

import jax
import jax.numpy as jnp
from jax import lax
from jax.experimental import pallas as pl
from jax.experimental.pallas import tpu as pltpu

F32 = jnp.float32
BF16 = jnp.bfloat16

D_MODEL = 2048
DEPTH = 4
N_A = DEPTH // 2
RET_DIM = 256
RET_HEADS = D_MODEL // RET_DIM
RET_WIDTH = RET_HEADS * RET_DIM
RET_CHUNK = 256
RET_THETA = 10000.0
SWA_DIM = 64
SWA_HEADS = D_MODEL // SWA_DIM
SWA_KV_HEADS = SWA_HEADS // 8
SWA_WIDTH = SWA_HEADS * SWA_DIM
SWA_KV_WIDTH = SWA_KV_HEADS * SWA_DIM
WINDOW = 128
ROPE_THETA = 500000.0
ROT_DIM = SWA_DIM // 4
MEM_HEADS = 4
MEM_DIM = D_MODEL // 8
MEM_WIDTH = MEM_HEADS * MEM_DIM
EPS = 1e-6
MASK_VALUE = -1e30
LOG2_E = 1.4426950408889634

LANES = 128
VMEM_LIMIT = 56 * 1024 * 1024

NT_DIMS = (((1,), (1,)), ((), ()))
TN_DIMS = (((0,), (0,)), ((), ()))

IN_TILE = 1024
PLAIN, SILU, ROPE_RET, ROPE_SWA = range(4)
TILES_A = ((ROPE_RET, (0, 1, 2, 3)), (PLAIN, (4, 5, 8)), (SILU, (6, 7, 9)))
TILES_B = ((ROPE_SWA, (0, 1)), (PLAIN, (4,)), (SILU, (2, 3, 5)))


def _params(n_axes):
    return pltpu.CompilerParams(dimension_semantics=("arbitrary",) * n_axes,
                                vmem_limit_bytes=VMEM_LIMIT)


def _silu(g):
    return g * (1.0 / (1.0 + jnp.exp(-g)))


def _rms(x):
    return x * lax.rsqrt(jnp.mean(x * x, axis=-1, keepdims=True) + EPS)


def _tables_kernel(pos_ref, inv_r_ref, inv_s_ref, sgn_ref, cr_ref, sr_ref, ck_ref, sk_ref, cq_ref, sq_ref):
    pos = pos_ref[...]
    ang_r = pos * inv_r_ref[...]
    cr_ref[...] = jnp.cos(ang_r)
    sr_ref[...] = jnp.sin(ang_r)
    ang_s = pos * inv_s_ref[...]
    cos_s = jnp.cos(ang_s)
    sin_s = jnp.sin(ang_s) * sgn_ref[...]
    ck_ref[...] = cos_s
    sk_ref[...] = sin_s
    cq_ref[...] = cos_s * (LOG2_E * SWA_DIM ** -0.5)
    sq_ref[...] = sin_s * (LOG2_E * SWA_DIM ** -0.5)


def _rotary_tables(pos, tm=1024):
    M = pos.shape[0]
    tm = min(tm, M)
    inv_r = RET_THETA ** (-jnp.arange(0, RET_DIM, 2, dtype=F32) / RET_DIM)
    inv_h = ROPE_THETA ** (-jnp.arange(0, ROT_DIM, 2, dtype=F32) / ROT_DIM)
    half = ROT_DIM // 2
    head = jnp.concatenate([inv_h, inv_h, jnp.zeros((SWA_DIM - ROT_DIM,), F32)])
    sgn_h = jnp.concatenate([-jnp.ones((half,), F32), jnp.ones((half,), F32),
                             jnp.zeros((SWA_DIM - ROT_DIM,), F32)])
    inv_s = jnp.tile(head, LANES // SWA_DIM)
    sgn = jnp.tile(sgn_h, LANES // SWA_DIM)
    row = pl.BlockSpec((1, LANES), lambda i: (0, 0))
    tab = pl.BlockSpec((tm, LANES), lambda i: (i, 0))
    shape = jax.ShapeDtypeStruct((M, LANES), F32)
    return pl.pallas_call(
        _tables_kernel, grid=(M // tm,),
        in_specs=[pl.BlockSpec((tm, 1), lambda i: (i, 0)), row, row, row],
        out_specs=[tab] * 6, out_shape=[shape] * 6,
        compiler_params=_params(1), name="rotary_tables",
    )(pos, inv_r[None], inv_s[None], sgn[None])


def _norm_kernel(x_ref, g_ref, o_ref):
    o_ref[...] = (_rms(x_ref[...]) * g_ref[...]).astype(o_ref.dtype)


def _rmsnorm_multi(x, gains, tm=512):
    M, D = x.shape
    G = gains.shape[0]
    tm = min(tm, M)
    return pl.pallas_call(
        _norm_kernel, grid=(M // tm, G),
        in_specs=[pl.BlockSpec((tm, D), lambda i, g: (i, 0)),
                  pl.BlockSpec((None, 1, D), lambda i, g: (g, 0, 0))],
        out_specs=pl.BlockSpec((None, tm, D), lambda i, g: (g, i, 0)),
        out_shape=jax.ShapeDtypeStruct((G, M, D), BF16),
        compiler_params=_params(2), name="rmsnorm",
    )(x, gains[:, None, :])


def _mem_kv_kernel(a_ref, w_ref, o_ref):
    acc = jnp.dot(a_ref[...], w_ref[...].astype(BF16), preferred_element_type=F32)
    scale = jnp.where(pl.program_id(1) * 2 < pl.num_programs(1), MEM_DIM ** -0.5, 1.0)
    o_ref[...] = (acc * scale).astype(o_ref.dtype)


def _mem_kv(mem_n, w, tn=512):
    L, R, D = mem_n.shape
    N = w.shape[-1]
    return pl.pallas_call(
        _mem_kv_kernel, grid=(L, N // tn),
        in_specs=[pl.BlockSpec((None, R, D), lambda l, j: (l, 0, 0)),
                  pl.BlockSpec((None, D, tn), lambda l, j: (l, 0, j))],
        out_specs=pl.BlockSpec((None, R, tn), lambda l, j: (l, 0, j)),
        out_shape=jax.ShapeDtypeStruct((L, R, N), BF16),
        compiler_params=_params(2), name="mem_kv",
    )(mem_n, w)


def _swa_rope(x, cos, sin_signed):
    lane = lax.broadcasted_iota(jnp.int32, x.shape, 1)
    first_half = (lane & (SWA_DIM - 1)) < (ROT_DIM // 2)
    partner = jnp.where(first_half, pltpu.roll(x, LANES - ROT_DIM // 2, 1), pltpu.roll(x, ROT_DIM // 2, 1))
    return x * cos + partner * sin_signed


IN_CHUNK = 256


def _in_proj_kernel(a_ref, w_ref, *rest, kind):
    o_ref, wb_ref = rest[-2:]

    @pl.when(pl.program_id(1) == 0)
    def _cast_weight():
        wb_ref[...] = w_ref[...].astype(BF16)

    for r in range(0, a_ref.shape[0], IN_CHUNK):
        rows = pl.ds(r, IN_CHUNK)
        acc = jnp.dot(a_ref[rows, :], wb_ref[...], preferred_element_type=F32)
        if kind == PLAIN:
            o_ref[rows, :] = acc.astype(o_ref.dtype)
        elif kind == SILU:
            o_ref[rows, :] = _silu(acc).astype(o_ref.dtype)
        elif kind == ROPE_RET:
            cos, sin = rest[0][rows, :], rest[1][rows, :]
            half = RET_DIM // 2
            for c in range(0, IN_TILE, RET_DIM):
                x1, x2 = acc[:, c:c + half], acc[:, c + half:c + RET_DIM]
                o_ref[rows, c:c + half] = (x1 * cos - x2 * sin).astype(o_ref.dtype)
                o_ref[rows, c + half:c + RET_DIM] = (x2 * cos + x1 * sin).astype(o_ref.dtype)
        else:
            cos, sin = rest[0][rows, :], rest[1][rows, :]
            for c in range(0, IN_TILE, LANES):
                o_ref[rows, c:c + LANES] = _swa_rope(acc[:, c:c + LANES], cos, sin).astype(o_ref.dtype)


def _in_proj(a, w, layer, kind, tiles, cos=None, sin=None, tm=1024):
    M, K = a.shape
    tm = min(tm, M)

    def tile_of(j):
        t = tiles[-1]
        for jj in range(len(tiles) - 1):
            t = jnp.where(j == jj, tiles[jj], t)
        return t

    in_specs = [pl.BlockSpec((tm, K), lambda j, i: (i, 0)),
                pl.BlockSpec((None, K, IN_TILE), lambda j, i: (layer, 0, tile_of(j)))]
    args = [a, w]
    if kind in (ROPE_RET, ROPE_SWA):
        in_specs += [pl.BlockSpec((tm, LANES), lambda j, i: (i, 0))] * 2
        args += [cos, sin]
    return pl.pallas_call(
        functools.partial(_in_proj_kernel, kind=kind), grid=(len(tiles), M // tm),
        in_specs=in_specs,
        out_specs=pl.BlockSpec((tm, IN_TILE), lambda j, i: (i, j)),
        out_shape=jax.ShapeDtypeStruct((M, len(tiles) * IN_TILE), BF16),
        scratch_shapes=[pltpu.VMEM((K, IN_TILE), BF16)],
        compiler_params=_params(2), name="in_proj",
    )(*args)


def _retention_kernel(q_ref, k_ref, v_ref, sg_ref, o_ref, r_ref, dec_ref, xi_ref, zeta_ref, gc_ref, *, n_chunks):
    C = RET_CHUNK
    h = pl.program_id(1)
    k_scale = RET_DIM ** -0.5

    @pl.when(pl.program_id(2) == 0)
    def _init():
        def log_gamma(shape):
            return jnp.log(1.0 - jnp.exp2(-5.0 - jnp.full(shape, h, jnp.int32).astype(F32)))

        r_ref[...] = jnp.zeros_like(r_ref)
        n = lax.broadcasted_iota(jnp.int32, (C, RET_DIM), 0).astype(F32)
        xi_ref[...] = jnp.exp(log_gamma((C, RET_DIM)) * (n + 1.0))
        zeta_ref[...] = jnp.exp(log_gamma((C, RET_DIM)) * (C - 1.0 - n)) * k_scale
        gc_ref[...] = jnp.exp(log_gamma(gc_ref.shape) * float(C))
        diff = (lax.broadcasted_iota(jnp.int32, (C, C), 0)
                - lax.broadcasted_iota(jnp.int32, (C, C), 1)).astype(F32)
        dec_ref[...] = jnp.where(diff >= 0, jnp.exp(log_gamma((C, C)) * jnp.maximum(diff, 0.0)) * k_scale, 0.0)

    def qk(c):
        rows = pl.ds(c * C, C)
        return lax.dot_general(q_ref[rows, :], k_ref[rows, :], NT_DIMS, preferred_element_type=F32)

    r = r_ref[...]
    inner_next = qk(0)
    for c in range(n_chunks):
        rows = pl.ds(c * C, C)
        inner = inner_next
        if c + 1 < n_chunks:
            inner_next = qk(c + 1)
        q, v = q_ref[rows, :], v_ref[rows, :]
        kz = (k_ref[rows, :].astype(F32) * zeta_ref[...]).astype(BF16)
        update = lax.dot_general(kz, v, TN_DIMS, preferred_element_type=F32)
        cross = jnp.dot(q, r.astype(BF16), preferred_element_type=F32)
        o = jnp.dot((inner * dec_ref[...]).astype(BF16), v, preferred_element_type=F32) + xi_ref[...] * cross
        r = r * gc_ref[0:1, :] + update
        o_ref[rows, :] = (_rms(o) * sg_ref[rows, :].astype(F32)).astype(o_ref.dtype)
    r_ref[...] = r


def _retention(z_qk, z_v, z_sg, batch, ts=2048):
    M = z_qk.shape[0]
    S = M // batch
    ts = min(ts, S)
    nt = S // ts
    H = RET_HEADS

    def col(off):
        return pl.BlockSpec((ts, RET_DIM), lambda b, h, t: (b * nt + t, off + h))

    return pl.pallas_call(
        functools.partial(_retention_kernel, n_chunks=ts // RET_CHUNK),
        grid=(batch, H, nt),
        in_specs=[col(0), col(H), col(0), col(0)],
        out_specs=pl.BlockSpec((ts, RET_DIM), lambda b, h, t: (b * nt + t, h)),
        out_shape=jax.ShapeDtypeStruct((M, RET_WIDTH), BF16),
        scratch_shapes=[pltpu.VMEM((RET_DIM, RET_DIM), F32),
                        pltpu.VMEM((RET_CHUNK, RET_CHUNK), F32),
                        pltpu.VMEM((RET_CHUNK, RET_DIM), F32),
                        pltpu.VMEM((RET_CHUNK, RET_DIM), F32),
                        pltpu.VMEM((8, RET_DIM), F32)],
        compiler_params=_params(3), name="retention",
    )(z_qk, z_qk, z_v, z_sg)


def _kv_kernel(y_ref, w_ref, cos_ref, sin_ref, k_ref, vt_ref, wb_ref):
    @pl.when(pl.program_id(0) == 0)
    def _cast_weight():
        wb_ref[...] = w_ref[...].astype(BF16)

    low = lax.broadcasted_iota(jnp.int32, (IN_CHUNK, LANES), 1) < SWA_DIM
    for r in range(0, y_ref.shape[0], IN_CHUNK):
        rows = pl.ds(r, IN_CHUNK)
        kv = jnp.dot(y_ref[rows, :], wb_ref[...], preferred_element_type=F32)
        cos, sin = cos_ref[rows, :], sin_ref[rows, :]
        for blk in range(SWA_KV_WIDTH // LANES):
            k = _swa_rope(kv[:, blk * LANES:(blk + 1) * LANES], cos, sin)
            swapped = pltpu.roll(k, SWA_DIM, 1)
            tiles = (jnp.where(low, k, 0.0), jnp.where(low, 0.0, swapped),
                     jnp.where(low, swapped, 0.0), jnp.where(low, 0.0, k))
            for i, t in enumerate(tiles):
                k_ref[rows, (4 * blk + i) * LANES:(4 * blk + i + 1) * LANES] = t.astype(k_ref.dtype)
        vt_ref[:, r:r + IN_CHUNK] = kv[:, SWA_KV_WIDTH:].T.astype(vt_ref.dtype)


SWA_K_COLS = 2 * LANES * SWA_KV_HEADS


def _shared_kv(y, w, cos, sin, tm=1024):
    M, D = y.shape
    tm = min(tm, M)
    tab = pl.BlockSpec((tm, LANES), lambda i: (i, 0))
    return pl.pallas_call(
        _kv_kernel, grid=(M // tm,),
        in_specs=[pl.BlockSpec((tm, D), lambda i: (i, 0)),
                  pl.BlockSpec((D, 2 * SWA_KV_WIDTH), lambda i: (0, 0)), tab, tab],
        out_specs=[pl.BlockSpec((tm, SWA_K_COLS), lambda i: (i, 0)),
                   pl.BlockSpec((SWA_KV_WIDTH, tm), lambda i: (0, i))],
        out_shape=[jax.ShapeDtypeStruct((M, SWA_K_COLS), BF16),
                   jax.ShapeDtypeStruct((SWA_KV_WIDTH, M), BF16)],
        scratch_shapes=[pltpu.VMEM((D, 2 * SWA_KV_WIDTH), BF16)],
        compiler_params=_params(1), name="shared_kv",
    )(y, w, cos, sin)


SWA_ONES_ROWS = 16


def _swa_kernel(sink_ref, q_ref, sg_ref, kp_ref, kc_ref, vtp_ref, vtc_ref, o_ref):
    W = WINDOW
    n_pairs = SWA_HEADS // SWA_KV_HEADS // 2
    QN = n_pairs * W
    jrow = lax.broadcasted_iota(jnp.int32, (W, QN), 0)
    icol = lax.broadcasted_iota(jnp.int32, (W, QN), 1)
    upper = jrow > (icol & (W - 1))
    upper_bf16 = jnp.where(upper, 1.0, 0.0).astype(BF16)
    pair_of_lane = lax.broadcasted_iota(jnp.int32, (1, QN), 1) // W
    ones_rows = jnp.ones((SWA_ONES_ROWS, W), BF16)

    def scores(blk, kh):
        rows, k_prev, _, k_cur, _ = blk
        key_tiles = []
        for parity in range(2):
            kcols = slice((2 * kh + parity) * LANES, (2 * kh + parity + 1) * LANES)
            key_tiles += ([k_prev(kcols)] if k_prev is not None else []) + [k_cur(kcols)]
        keys = jnp.concatenate(key_tiles, axis=0)
        q = jnp.concatenate([q_ref[rows, (kh * n_pairs + j) * LANES:(kh * n_pairs + j + 1) * LANES]
                             for j in range(n_pairs)], axis=0)
        return lax.dot_general(keys, q, NT_DIMS, preferred_element_type=F32)

    def attend(blk, kh, st):
        rows, k_prev, vt_prev, _, vt_cur = blk
        has_prev = k_prev is not None
        nk = 2 * W if has_prev else W
        vrows = slice(kh * SWA_DIM, (kh + 1) * SWA_DIM)
        vt = jnp.concatenate([vt_cur(vrows), ones_rows], axis=0)
        if has_prev:
            vt = jnp.concatenate([jnp.concatenate([vt_prev(vrows), ones_rows], axis=0), vt], axis=1)
        pairs = [kh * n_pairs + j for j in range(n_pairs)]
        ot = []
        for parity in range(2):
            sp = st[parity * nk:(parity + 1) * nk]
            s = jnp.where(upper, sp[:W], sp[W:]) if has_prev else jnp.where(upper, MASK_VALUE, sp)
            sink = jnp.full((1, QN), sink_ref[2 * pairs[-1] + parity], F32)
            for j in range(n_pairs - 1):
                sink = jnp.where(pair_of_lane == j, sink_ref[2 * pairs[j] + parity], sink)
            sink = sink * LOG2_E
            m = jnp.maximum(jnp.max(s, axis=0, keepdims=True), sink)
            e = jnp.exp2(s - m).astype(BF16)
            if has_prev:
                e_prev = e * upper_bf16
                e = jnp.concatenate([e_prev, e - e_prev], axis=0)
            acc = jnp.dot(vt, e, preferred_element_type=F32)
            denom = acc[SWA_DIM:SWA_DIM + 1] + jnp.exp2(sink - m)
            ot.append(acc[:SWA_DIM] / denom)
        for j, p in enumerate(pairs):
            cols = slice(p * LANES, (p + 1) * LANES)
            o = jnp.concatenate([ot[0][:, j * W:(j + 1) * W], ot[1][:, j * W:(j + 1) * W]], axis=0).T
            o_ref[rows, cols] = (o * sg_ref[rows, cols].astype(F32)).astype(o_ref.dtype)

    def step(blocks):
        units = [(blk, kh) for blk in blocks for kh in range(SWA_KV_HEADS)]
        st = scores(*units[0])
        for u, unit in enumerate(units):
            st_next = scores(*units[u + 1]) if u + 1 < len(units) else None
            attend(*unit, st)
            st = st_next

    first, second = pl.ds(0, W), pl.ds(W, W)
    k_first = lambda c: kc_ref[first, c]
    vt_first = lambda r: vtc_ref[r, 0:W]
    k_second = lambda c: kc_ref[second, c]
    vt_second = lambda r: vtc_ref[r, W:2 * W]
    k_before = lambda c: kp_ref[:, c]
    vt_before = lambda r: vtp_ref[r, :]
    second_block = (second, k_first, vt_first, k_second, vt_second)
    n = pl.program_id(1)
    pl.when(n == 0)(lambda: step([(first, None, None, k_first, vt_first), second_block]))
    pl.when(n > 0)(lambda: step([(first, k_before, vt_before, k_first, vt_first), second_block]))


def _swa(z_q, z_sg, k2, vt, sinks, batch):
    M = z_q.shape[0]
    W2 = 2 * WINDOW
    nb = M // batch // W2

    def rows(width):
        return pl.BlockSpec((W2, width), lambda b, n: (b * nb + n, 0))

    before = lambda b, n: 2 * (b * nb + n) - jnp.where(n > 0, 1, 0)
    return pl.pallas_call(
        _swa_kernel, grid=(batch, nb),
        in_specs=[pl.BlockSpec(memory_space=pltpu.SMEM),
                  rows(SWA_WIDTH), rows(SWA_WIDTH),
                  pl.BlockSpec((WINDOW, SWA_K_COLS), lambda b, n: (before(b, n), 0)), rows(SWA_K_COLS),
                  pl.BlockSpec((SWA_KV_WIDTH, WINDOW), lambda b, n: (0, before(b, n))),
                  pl.BlockSpec((SWA_KV_WIDTH, W2), lambda b, n: (0, b * nb + n))],
        out_specs=rows(SWA_WIDTH),
        out_shape=jax.ShapeDtypeStruct((M, SWA_WIDTH), BF16),
        compiler_params=_params(2), name="swa",
    )(sinks, z_q, z_sg, k2, k2, vt, vt)


OUT_CHUNK = 128


def _out_kernel(a_ref, qm_ref, sgm_ref, mk_ref, mv_ref, w1_ref, w2_ref, h_ref, pg_ref, *rest, n_next):
    ng_ref = rest[0] if n_next else None
    h_out_ref = rest[1] if n_next else rest[0]
    y_refs = rest[2:2 + n_next]
    om_ref = rest[-1]
    tm = a_ref.shape[0]
    heads = [slice(hh * MEM_DIM, (hh + 1) * MEM_DIM) for hh in range(MEM_HEADS)]
    chunks = [pl.ds(r, OUT_CHUNK) for r in range(0, tm, OUT_CHUNK)]

    scores = [lax.dot_general(qm_ref[:, cols], mk_ref[:, cols], NT_DIMS, preferred_element_type=F32)
              for cols in heads]
    acc = jnp.dot(a_ref[chunks[0], :], w1_ref[...], preferred_element_type=F32)
    for cols, s in zip(heads, scores):
        e = jnp.exp(s - jnp.max(s, axis=-1, keepdims=True))
        p = (e / jnp.sum(e, axis=-1, keepdims=True)).astype(BF16)
        o = jnp.dot(p, mv_ref[:, cols], preferred_element_type=F32)
        om_ref[:, cols] = (o * sgm_ref[:, cols].astype(F32)).astype(om_ref.dtype)
    for c, rows in enumerate(chunks):
        if c:
            acc = jnp.dot(a_ref[rows, :], w1_ref[...], preferred_element_type=F32)
        acc = acc + jnp.dot(om_ref[rows, :], w2_ref[...], preferred_element_type=F32)
        h = h_ref[rows, :] + _rms(acc) * pg_ref[...]
        h_out_ref[rows, :] = h
        if n_next:
            hn = _rms(h)
            for g in range(n_next):
                y_refs[g][rows, :] = (hn * ng_ref[g]).astype(y_refs[g].dtype)


def _mix_out(o_main, z_qm, z_sgm, mkv, w_out, layer, h, post_g, next_g, batch, tm=512):
    M = h.shape[0]
    tm = min(tm, M)
    tiles_per_batch = M // batch // tm
    Mm = mkv.shape[1] // batch
    n_next = len(next_g)
    k1 = o_main.shape[1]
    assert k1 % MEM_WIDTH == 0

    def row(width, off=0):
        return pl.BlockSpec((tm, width), lambda i: (i, off))

    def mem(off):
        return pl.BlockSpec((None, Mm, MEM_WIDTH), lambda i: (layer, i // tiles_per_batch, off))

    once = pl.Buffered(1)
    in_specs = [row(k1), row(MEM_WIDTH, z_qm.shape[1] // MEM_WIDTH - 1),
                row(MEM_WIDTH, z_sgm.shape[1] // MEM_WIDTH - 1), mem(0), mem(1),
                pl.BlockSpec((None, k1, D_MODEL), lambda i: (layer, 0, 0), pipeline_mode=once),
                pl.BlockSpec((None, MEM_WIDTH, D_MODEL), lambda i: (layer, k1 // MEM_WIDTH, 0), pipeline_mode=once),
                row(D_MODEL),
                pl.BlockSpec((1, D_MODEL), lambda i: (0, 0))]
    args = [o_main, z_qm, z_sgm, mkv, mkv, w_out, w_out, h, post_g[None]]
    if n_next:
        in_specs.append(pl.BlockSpec((n_next, 1, D_MODEL), lambda i: (0, 0, 0)))
        args.append(jnp.stack(next_g)[:, None, :])
    out_specs = [row(D_MODEL)] * (1 + n_next)
    out_shape = [jax.ShapeDtypeStruct((M, D_MODEL), F32)] + [jax.ShapeDtypeStruct((M, D_MODEL), BF16)] * n_next
    res = pl.pallas_call(
        functools.partial(_out_kernel, n_next=n_next), grid=(M // tm,),
        in_specs=in_specs, out_specs=out_specs, out_shape=out_shape,
        scratch_shapes=[pltpu.VMEM((tm, MEM_WIDTH), BF16)],
        compiler_params=_params(1), name="mix_out",
    )(*args)
    return res[0], res[1:]


def kernel(x, mem, positions, pre_norm_g, post_norm_g, mem_norm_g, kv_norm_g,
           w_in_a, w_in_b, w_kv_b, sinks_b, w_mem_kv, w_out):
    B, S, D = x.shape
    Mm = mem.shape[1]
    M = B * S
    w_out = w_out.astype(BF16)

    cos_r, sin_r, cos_k, sin_k, cos_q, sin_q = _rotary_tables(positions.astype(F32).reshape(M, 1))
    mkv = _mem_kv(_rmsnorm_multi(mem.reshape(B * Mm, D), mem_norm_g), w_mem_kv)
    h = x.reshape(M, D)
    y = _rmsnorm_multi(h, pre_norm_g[:1]).reshape(M, D)
    k2 = vt = None
    for l in range(DEPTH):
        if l < N_A:
            z_rope, z_plain, z_silu = (_in_proj(y, w_in_a, l, kind, tiles, cos_r, sin_r) for kind, tiles in TILES_A)
            o_main = _retention(z_rope, z_plain, z_silu, B)
        else:
            z_rope, z_plain, z_silu = (_in_proj(y, w_in_b, l - N_A, kind, tiles, cos_q, sin_q)
                                       for kind, tiles in TILES_B)
            o_main = _swa(z_rope, z_silu, k2, vt, sinks_b[l - N_A], B)
        next_g = [pre_norm_g[l + 1]] if l < DEPTH - 1 else []
        if l == N_A - 1:
            next_g.append(kv_norm_g)
        h, ys = _mix_out(o_main, z_plain, z_silu, mkv, w_out, l, h, post_norm_g[l], next_g, B)
        if ys:
            y = ys[0]
        if l == N_A - 1:
            k2, vt = _shared_kv(ys[1], w_kv_b, cos_k, sin_k)
    return h.reshape(B, S, D)
```

```python
import functools

import jax
import jax.numpy as jnp
from jax import lax
from jax.experimental import pallas as pl
from jax.experimental.pallas import tpu as pltpu

F32 = jnp.float32
BF16 = jnp.bfloat16

D_MODEL = 2048
DEPTH = 4
N_A = DEPTH // 2
RET_DIM = 256
RET_HEADS = D_MODEL // RET_DIM
RET_WIDTH = RET_HEADS * RET_DIM
RET_CHUNK = 256
RET_THETA = 10000.0
SWA_DIM = 64
SWA_HEADS = D_MODEL // SWA_DIM
SWA_KV_HEADS = SWA_HEADS // 8
SWA_WIDTH = SWA_HEADS * SWA_DIM
SWA_KV_WIDTH = SWA_KV_HEADS * SWA_DIM
WINDOW = 128
ROPE_THETA = 500000.0
ROT_DIM = SWA_DIM // 4
MEM_HEADS = 4
MEM_DIM = D_MODEL // 8
MEM_WIDTH = MEM_HEADS * MEM_DIM
EPS = 1e-6
MASK_VALUE = -1e30
LOG2_E = 1.4426950408889634

LANES = 128
VMEM_LIMIT = 56 * 1024 * 1024

NT_DIMS = (((1,), (1,)), ((), ()))
TN_DIMS = (((0,), (0,)), ((), ()))

IN_TILE = 1024
PLAIN, SILU = range(2)
TILES_A = ((PLAIN, (8,)), (SILU, (9,)))
TILES_B = ((PLAIN, (4,)), (SILU, (5,)))


def _params(n_axes):
    return pltpu.CompilerParams(dimension_semantics=("arbitrary",) * n_axes,
                                vmem_limit_bytes=VMEM_LIMIT)


def _silu(g):
    return g * (1.0 / (1.0 + jnp.exp(-g)))


def _rms(x):
    return x * lax.rsqrt(jnp.mean(x * x, axis=-1, keepdims=True) + EPS)


def _tables_kernel(pos_ref, inv_r_ref, inv_s_ref, sgn_ref, cr_ref, sr_ref, ck_ref, sk_ref, cq_ref, sq_ref):
    pos = pos_ref[...]
    ang_r = pos * inv_r_ref[...]
    cr_ref[...] = jnp.cos(ang_r)
    sr_ref[...] = jnp.sin(ang_r)
    ang_s = pos * inv_s_ref[...]
    cos_s = jnp.cos(ang_s)
    sin_s = jnp.sin(ang_s) * sgn_ref[...]
    ck_ref[...] = cos_s
    sk_ref[...] = sin_s
    cq_ref[...] = cos_s * (LOG2_E * SWA_DIM ** -0.5)
    sq_ref[...] = sin_s * (LOG2_E * SWA_DIM ** -0.5)


def _rotary_tables(pos, tm=1024):
    M = pos.shape[0]
    tm = min(tm, M)
    inv_r = RET_THETA ** (-jnp.arange(0, RET_DIM, 2, dtype=F32) / RET_DIM)
    inv_h = ROPE_THETA ** (-jnp.arange(0, ROT_DIM, 2, dtype=F32) / ROT_DIM)
    half = ROT_DIM // 2
    head = jnp.concatenate([inv_h, inv_h, jnp.zeros((SWA_DIM - ROT_DIM,), F32)])
    sgn_h = jnp.concatenate([-jnp.ones((half,), F32), jnp.ones((half,), F32),
                             jnp.zeros((SWA_DIM - ROT_DIM,), F32)])
    inv_s = jnp.tile(head, LANES // SWA_DIM)
    sgn = jnp.tile(sgn_h, LANES // SWA_DIM)
    row = pl.BlockSpec((1, LANES), lambda i: (0, 0))
    tab = pl.BlockSpec((tm, LANES), lambda i: (i, 0))
    shape = jax.ShapeDtypeStruct((M, LANES), F32)
    return pl.pallas_call(
        _tables_kernel, grid=(M // tm,),
        in_specs=[pl.BlockSpec((tm, 1), lambda i: (i, 0)), row, row, row],
        out_specs=[tab] * 6, out_shape=[shape] * 6,
        compiler_params=_params(1), name="rotary_tables",
    )(pos, inv_r[None], inv_s[None], sgn[None])


def _norm_kernel(x_ref, g_ref, o_ref):
    o_ref[...] = (_rms(x_ref[...]) * g_ref[...]).astype(o_ref.dtype)


def _rmsnorm_multi(x, gains, tm=512):
    M, D = x.shape
    G = gains.shape[0]
    tm = min(tm, M)
    return pl.pallas_call(
        _norm_kernel, grid=(M // tm, G),
        in_specs=[pl.BlockSpec((tm, D), lambda i, g: (i, 0)),
                  pl.BlockSpec((None, 1, D), lambda i, g: (g, 0, 0))],
        out_specs=pl.BlockSpec((None, tm, D), lambda i, g: (g, i, 0)),
        out_shape=jax.ShapeDtypeStruct((G, M, D), BF16),
        compiler_params=_params(2), name="rmsnorm",
    )(x, gains[:, None, :])


def _mem_kv_kernel(a_ref, w_ref, o_ref):
    acc = jnp.dot(a_ref[...], w_ref[...].astype(BF16), preferred_element_type=F32)
    scale = jnp.where(pl.program_id(1) * 2 < pl.num_programs(1), MEM_DIM ** -0.5, 1.0)
    o_ref[...] = (acc * scale).astype(o_ref.dtype)


def _mem_kv(mem_n, w, tn=512):
    L, R, D = mem_n.shape
    N = w.shape[-1]
    return pl.pallas_call(
        _mem_kv_kernel, grid=(L, N // tn),
        in_specs=[pl.BlockSpec((None, R, D), lambda l, j: (l, 0, 0)),
                  pl.BlockSpec((None, D, tn), lambda l, j: (l, 0, j))],
        out_specs=pl.BlockSpec((None, R, tn), lambda l, j: (l, 0, j)),
        out_shape=jax.ShapeDtypeStruct((L, R, N), BF16),
        compiler_params=_params(2), name="mem_kv",
    )(mem_n, w)


def _swa_rope(x, cos, sin_signed):
    lane = lax.broadcasted_iota(jnp.int32, x.shape, 1)
    first_half = (lane & (SWA_DIM - 1)) < (ROT_DIM // 2)
    partner = jnp.where(first_half, pltpu.roll(x, LANES - ROT_DIM // 2, 1), pltpu.roll(x, ROT_DIM // 2, 1))
    return x * cos + partner * sin_signed


IN_CHUNK = 256


def _in_proj_kernel(a_ref, w_ref, o_ref, wb_ref, *, kind):
    @pl.when(pl.program_id(1) == 0)
    def _cast_weight():
        wb_ref[...] = w_ref[...].astype(BF16)

    for r in range(0, a_ref.shape[0], IN_CHUNK):
        rows = pl.ds(r, IN_CHUNK)
        acc = jnp.dot(a_ref[rows, :], wb_ref[...], preferred_element_type=F32)
        o_ref[rows, :] = (_silu(acc) if kind == SILU else acc).astype(o_ref.dtype)


def _in_proj(a, w, layer, kind, tiles, tm=1024):
    M, K = a.shape
    tm = min(tm, M)

    def tile_of(j):
        t = tiles[-1]
        for jj in range(len(tiles) - 1):
            t = jnp.where(j == jj, tiles[jj], t)
        return t

    return pl.pallas_call(
        functools.partial(_in_proj_kernel, kind=kind), grid=(len(tiles), M // tm),
        in_specs=[pl.BlockSpec((tm, K), lambda j, i: (i, 0)),
                  pl.BlockSpec((None, K, IN_TILE), lambda j, i: (layer, 0, tile_of(j)))],
        out_specs=pl.BlockSpec((tm, IN_TILE), lambda j, i: (i, j)),
        out_shape=jax.ShapeDtypeStruct((M, len(tiles) * IN_TILE), BF16),
        scratch_shapes=[pltpu.VMEM((K, IN_TILE), BF16)],
        compiler_params=_params(2), name="in_proj",
    )(a, w)


def _retention_kernel(y_ref, wq_ref, wk_ref, wv_ref, wg_ref, cos_ref, sin_ref, o_ref,
                      wb_ref, r_ref, dec_ref, xi_ref, zeta_ref, gc_ref, *, tiles_per_seq):
    C = RET_CHUNK
    D = RET_DIM
    h = pl.program_id(0)
    i = pl.program_id(1)
    n_chunks = y_ref.shape[0] // C
    k_scale = RET_DIM ** -0.5

    @pl.when(i == 0)
    def _init():
        def log_gamma(shape):
            return jnp.log(1.0 - jnp.exp2(-5.0 - jnp.full(shape, h, jnp.int32).astype(F32)))

        for c, w_ref in enumerate((wq_ref, wk_ref, wv_ref, wg_ref)):
            wb_ref[:, c * D:(c + 1) * D] = w_ref[...].astype(BF16)
        n = lax.broadcasted_iota(jnp.int32, (C, RET_DIM), 0).astype(F32)
        xi_ref[...] = jnp.exp(log_gamma((C, RET_DIM)) * (n + 1.0))
        zeta_ref[...] = jnp.exp(log_gamma((C, RET_DIM)) * (C - 1.0 - n)) * k_scale
        gc_ref[...] = jnp.exp(log_gamma(gc_ref.shape) * float(C))
        diff = (lax.broadcasted_iota(jnp.int32, (C, C), 0)
                - lax.broadcasted_iota(jnp.int32, (C, C), 1)).astype(F32)
        dec_ref[...] = jnp.where(diff >= 0, jnp.exp(log_gamma((C, C)) * jnp.maximum(diff, 0.0)) * k_scale, 0.0)

    @pl.when(i % tiles_per_seq == 0)
    def _reset_state():
        r_ref[...] = jnp.zeros_like(r_ref)

    half = D // 2

    def rope(x, cos, sin):
        x1, x2 = x[:, :half], x[:, half:]
        return jnp.concatenate([x1 * cos - x2 * sin, x2 * cos + x1 * sin], axis=-1)

    def project(c):
        rows = pl.ds(c * C, C)
        z = jnp.dot(y_ref[rows, :], wb_ref[...], preferred_element_type=F32)
        cos, sin = cos_ref[rows, :], sin_ref[rows, :]
        k = rope(z[:, D:2 * D], cos, sin)
        return (rope(z[:, :D], cos, sin).astype(BF16), k.astype(BF16), (k * zeta_ref[...]).astype(BF16),
                z[:, 2 * D:3 * D].astype(BF16), _silu(z[:, 3 * D:]).astype(BF16))

    r = r_ref[...]
    nxt = project(0)
    for c in range(n_chunks):
        q, k, kz, v, sg = nxt
        if c + 1 < n_chunks:
            nxt = project(c + 1)
        inner = lax.dot_general(q, k, NT_DIMS, preferred_element_type=F32)
        update = lax.dot_general(kz, v, TN_DIMS, preferred_element_type=F32)
        cross = jnp.dot(q, r.astype(BF16), preferred_element_type=F32)
        o = jnp.dot((inner * dec_ref[...]).astype(BF16), v, preferred_element_type=F32) + xi_ref[...] * cross
        r = r * gc_ref[0:1, :] + update
        o_ref[pl.ds(c * C, C), :] = (_rms(o) * sg.astype(F32)).astype(o_ref.dtype)
    r_ref[...] = r


def _retention(y, w, layer, cos, sin, batch, tm=2048):
    M, K = y.shape
    S = M // batch
    tm = min(tm, S)
    H = RET_HEADS

    def wcol(group):
        return pl.BlockSpec((None, K, RET_DIM), lambda h, i: (layer, 0, group * H + h))

    tab = pl.BlockSpec((tm, LANES), lambda h, i: (i, 0))
    return pl.pallas_call(
        functools.partial(_retention_kernel, tiles_per_seq=S // tm),
        grid=(H, M // tm),
        in_specs=[pl.BlockSpec((tm, K), lambda h, i: (i, 0)), wcol(0), wcol(1), wcol(2), wcol(3), tab, tab],
        out_specs=pl.BlockSpec((tm, RET_DIM), lambda h, i: (i, h)),
        out_shape=jax.ShapeDtypeStruct((M, RET_WIDTH), BF16),
        scratch_shapes=[pltpu.VMEM((K, 4 * RET_DIM), BF16),
                        pltpu.VMEM((RET_DIM, RET_DIM), F32),
                        pltpu.VMEM((RET_CHUNK, RET_CHUNK), F32),
                        pltpu.VMEM((RET_CHUNK, RET_DIM), F32),
                        pltpu.VMEM((RET_CHUNK, RET_DIM), F32),
                        pltpu.VMEM((8, RET_DIM), F32)],
        compiler_params=_params(2), name="retention",
    )(y, w, w, w, w, cos, sin)


def _kv_kernel(y_ref, w_ref, cos_ref, sin_ref, k_ref, vt_ref, wb_ref):
    @pl.when(pl.program_id(0) == 0)
    def _cast_weight():
        wb_ref[...] = w_ref[...].astype(BF16)

    low = lax.broadcasted_iota(jnp.int32, (IN_CHUNK, LANES), 1) < SWA_DIM
    for r in range(0, y_ref.shape[0], IN_CHUNK):
        rows = pl.ds(r, IN_CHUNK)
        kv = jnp.dot(y_ref[rows, :], wb_ref[...], preferred_element_type=F32)
        cos, sin = cos_ref[rows, :], sin_ref[rows, :]
        for blk in range(SWA_KV_WIDTH // LANES):
            k = _swa_rope(kv[:, blk * LANES:(blk + 1) * LANES], cos, sin)
            swapped = pltpu.roll(k, SWA_DIM, 1)
            tiles = (jnp.where(low, k, 0.0), jnp.where(low, 0.0, swapped),
                     jnp.where(low, swapped, 0.0), jnp.where(low, 0.0, k))
            for i, t in enumerate(tiles):
                k_ref[rows, (4 * blk + i) * LANES:(4 * blk + i + 1) * LANES] = t.astype(k_ref.dtype)
        vt_ref[:, r:r + IN_CHUNK] = kv[:, SWA_KV_WIDTH:].T.astype(vt_ref.dtype)


SWA_K_COLS = 2 * LANES * SWA_KV_HEADS


def _shared_kv(y, w, cos, sin, tm=1024):
    M, D = y.shape
    tm = min(tm, M)
    tab = pl.BlockSpec((tm, LANES), lambda i: (i, 0))
    return pl.pallas_call(
        _kv_kernel, grid=(M // tm,),
        in_specs=[pl.BlockSpec((tm, D), lambda i: (i, 0)),
                  pl.BlockSpec((D, 2 * SWA_KV_WIDTH), lambda i: (0, 0)), tab, tab],
        out_specs=[pl.BlockSpec((tm, SWA_K_COLS), lambda i: (i, 0)),
                   pl.BlockSpec((SWA_KV_WIDTH, tm), lambda i: (0, i))],
        out_shape=[jax.ShapeDtypeStruct((M, SWA_K_COLS), BF16),
                   jax.ShapeDtypeStruct((SWA_KV_WIDTH, M), BF16)],
        scratch_shapes=[pltpu.VMEM((D, 2 * SWA_KV_WIDTH), BF16)],
        compiler_params=_params(1), name="shared_kv",
    )(y, w, cos, sin)


SWA_ONES_ROWS = 16


SWA_GROUP = SWA_WIDTH // SWA_KV_HEADS


def _swa_kernel(sink_ref, y_ref, wq_ref, wg_ref, cos_ref, sin_ref, kp_ref, kc_ref, vtp_ref, vtc_ref, o_ref,
                wb_ref, *, tiles_per_seq):
    W = WINDOW
    n_pairs = SWA_GROUP // LANES
    QN = n_pairs * W
    g = pl.program_id(0)
    i = pl.program_id(1)
    n_chunks = y_ref.shape[0] // IN_CHUNK
    blocks_per_chunk = IN_CHUNK // W

    @pl.when(i == 0)
    def _cast_weights():
        wb_ref[:, :SWA_GROUP] = wq_ref[...].astype(BF16)
        wb_ref[:, SWA_GROUP:] = wg_ref[...].astype(BF16)

    jrow = lax.broadcasted_iota(jnp.int32, (W, QN), 0)
    icol = lax.broadcasted_iota(jnp.int32, (W, QN), 1)
    upper = jrow > (icol & (W - 1))
    upper_bf16 = jnp.where(upper, 1.0, 0.0).astype(BF16)
    pair_of_lane = lax.broadcasted_iota(jnp.int32, (1, QN), 1) // W
    ones_rows = jnp.ones((SWA_ONES_ROWS, W), BF16)
    no_prev_bias = jnp.where(i % tiles_per_seq == 0, MASK_VALUE, 0.0)

    def project(c):
        rows = pl.ds(c * IN_CHUNK, IN_CHUNK)
        z = jnp.dot(y_ref[rows, :], wb_ref[...], preferred_element_type=F32)
        cos, sin = cos_ref[rows, :], sin_ref[rows, :]
        q = [_swa_rope(z[:, j * LANES:(j + 1) * LANES], cos, sin).astype(BF16) for j in range(n_pairs)]
        sg = _silu(z[:, SWA_GROUP:]).astype(BF16)
        return [(jnp.concatenate([qj[b * W:(b + 1) * W] for qj in q], axis=0), sg[b * W:(b + 1) * W])
                for b in range(blocks_per_chunk)]

    def keys_of(t, parity):
        kcols = slice(parity * LANES, (parity + 1) * LANES)
        prev = kp_ref[:, kcols] if t == 0 else kc_ref[pl.ds((t - 1) * W, W), kcols]
        return [prev, kc_ref[pl.ds(t * W, W), kcols]]

    def scores(t, q):
        keys = jnp.concatenate(keys_of(t, 0) + keys_of(t, 1), axis=0)
        return lax.dot_general(keys, q, NT_DIMS, preferred_element_type=F32)

    def softmax(t, st):
        out = []
        for parity in range(2):
            sp = st[parity * 2 * W:(parity + 1) * 2 * W]
            s_prev = sp[:W] + no_prev_bias if t == 0 else sp[:W]
            s = jnp.where(upper, s_prev, sp[W:])
            sink = jnp.full((1, QN), sink_ref[2 * n_pairs * g + 2 * (n_pairs - 1) + parity], F32)
            for j in range(n_pairs - 1):
                sink = jnp.where(pair_of_lane == j, sink_ref[2 * n_pairs * g + 2 * j + parity], sink)
            sink = sink * LOG2_E
            m = jnp.maximum(jnp.max(s, axis=0, keepdims=True), sink)
            e = jnp.exp2(s - m).astype(BF16)
            e_prev = e * upper_bf16
            out.append((jnp.concatenate([e_prev, e - e_prev], axis=0), jnp.exp2(sink - m)))
        return out

    def finish(t, probs, sg):
        vt_prev = vtp_ref[...] if t == 0 else vtc_ref[:, (t - 1) * W:t * W]
        vt = jnp.concatenate([jnp.concatenate([vt_prev, ones_rows], axis=0),
                              jnp.concatenate([vtc_ref[:, t * W:(t + 1) * W], ones_rows], axis=0)], axis=1)
        ot = []
        for e, sink_term in probs:
            acc = jnp.dot(vt, e, preferred_element_type=F32)
            ot.append(acc[:SWA_DIM] / (acc[SWA_DIM:SWA_DIM + 1] + sink_term))
        for j in range(n_pairs):
            cols = slice(j * LANES, (j + 1) * LANES)
            o = jnp.concatenate([ot[0][:, j * W:(j + 1) * W], ot[1][:, j * W:(j + 1) * W]], axis=0).T
            o_ref[pl.ds(t * W, W), cols] = (o * sg[:, cols].astype(F32)).astype(o_ref.dtype)

    blocks = project(0)
    for c in range(n_chunks):
        ts = [c * blocks_per_chunk + b for b in range(blocks_per_chunk)]
        sts = [scores(t, q) for t, (q, _) in zip(ts, blocks)]
        nxt = project(c + 1) if c + 1 < n_chunks else None
        probs = [softmax(t, st) for t, st in zip(ts, sts)]
        for t, p, (_, sg) in zip(ts, probs, blocks):
            finish(t, p, sg)
        blocks = nxt


def _swa(y, w, layer, cos, sin, k2, vt, sinks, batch, tm=2048):
    M, K = y.shape
    S = M // batch
    tm = min(tm, S)
    n_groups = SWA_KV_HEADS
    blocks_per_tile = tm // WINDOW

    def before(i):
        return jnp.maximum(i * blocks_per_tile - 1, 0)

    tab = pl.BlockSpec((tm, LANES), lambda g, i: (i, 0))
    return pl.pallas_call(
        functools.partial(_swa_kernel, tiles_per_seq=S // tm),
        grid=(n_groups, M // tm),
        in_specs=[pl.BlockSpec(memory_space=pltpu.SMEM),
                  pl.BlockSpec((tm, K), lambda g, i: (i, 0)),
                  pl.BlockSpec((None, K, SWA_GROUP), lambda g, i: (layer, 0, g)),
                  pl.BlockSpec((None, K, SWA_GROUP), lambda g, i: (layer, 0, n_groups + g)),
                  tab, tab,
                  pl.BlockSpec((WINDOW, 2 * LANES), lambda g, i: (before(i), g)),
                  pl.BlockSpec((tm, 2 * LANES), lambda g, i: (i, g)),
                  pl.BlockSpec((SWA_DIM, WINDOW), lambda g, i: (g, before(i))),
                  pl.BlockSpec((SWA_DIM, tm), lambda g, i: (g, i))],
        out_specs=pl.BlockSpec((tm, SWA_GROUP), lambda g, i: (i, g)),
        out_shape=jax.ShapeDtypeStruct((M, SWA_WIDTH), BF16),
        scratch_shapes=[pltpu.VMEM((K, 2 * SWA_GROUP), BF16)],
        compiler_params=_params(2), name="swa",
    )(sinks, y, w, w, cos, sin, k2, k2, vt, vt)


OUT_CHUNK = 128


def _out_kernel(a_ref, qm_ref, sgm_ref, mk_ref, mv_ref, w1_ref, w2_ref, h_ref, pg_ref, *rest, n_next):
    ng_ref = rest[0] if n_next else None
    h_out_ref = rest[1] if n_next else rest[0]
    y_refs = rest[2:2 + n_next]
    om_ref = rest[-1]
    tm = a_ref.shape[0]
    heads = [slice(hh * MEM_DIM, (hh + 1) * MEM_DIM) for hh in range(MEM_HEADS)]
    chunks = [pl.ds(r, OUT_CHUNK) for r in range(0, tm, OUT_CHUNK)]

    scores = [lax.dot_general(qm_ref[:, cols], mk_ref[:, cols], NT_DIMS, preferred_element_type=F32)
              for cols in heads]
    acc = jnp.dot(a_ref[chunks[0], :], w1_ref[...], preferred_element_type=F32)
    for cols, s in zip(heads, scores):
        e = jnp.exp(s - jnp.max(s, axis=-1, keepdims=True))
        p = (e / jnp.sum(e, axis=-1, keepdims=True)).astype(BF16)
        o = jnp.dot(p, mv_ref[:, cols], preferred_element_type=F32)
        om_ref[:, cols] = (o * sgm_ref[:, cols].astype(F32)).astype(om_ref.dtype)
    for c, rows in enumerate(chunks):
        if c:
            acc = jnp.dot(a_ref[rows, :], w1_ref[...], preferred_element_type=F32)
        acc = acc + jnp.dot(om_ref[rows, :], w2_ref[...], preferred_element_type=F32)
        h = h_ref[rows, :] + _rms(acc) * pg_ref[...]
        h_out_ref[rows, :] = h
        if n_next:
            hn = _rms(h)
            for g in range(n_next):
                y_refs[g][rows, :] = (hn * ng_ref[g]).astype(y_refs[g].dtype)


def _mix_out(o_main, z_qm, z_sgm, mkv, w_out, layer, h, post_g, next_g, batch, tm=512):
    M = h.shape[0]
    tm = min(tm, M)
    tiles_per_batch = M // batch // tm
    Mm = mkv.shape[1] // batch
    n_next = len(next_g)
    k1 = o_main.shape[1]
    assert k1 % MEM_WIDTH == 0

    def row(width, off=0):
        return pl.BlockSpec((tm, width), lambda i: (i, off))

    def mem(off):
        return pl.BlockSpec((None, Mm, MEM_WIDTH), lambda i: (layer, i // tiles_per_batch, off))

    once = pl.Buffered(1)
    in_specs = [row(k1), row(MEM_WIDTH, z_qm.shape[1] // MEM_WIDTH - 1),
                row(MEM_WIDTH, z_sgm.shape[1] // MEM_WIDTH - 1), mem(0), mem(1),
                pl.BlockSpec((None, k1, D_MODEL), lambda i: (layer, 0, 0), pipeline_mode=once),
                pl.BlockSpec((None, MEM_WIDTH, D_MODEL), lambda i: (layer, k1 // MEM_WIDTH, 0), pipeline_mode=once),
                row(D_MODEL),
                pl.BlockSpec((1, D_MODEL), lambda i: (0, 0))]
    args = [o_main, z_qm, z_sgm, mkv, mkv, w_out, w_out, h, post_g[None]]
    if n_next:
        in_specs.append(pl.BlockSpec((n_next, 1, D_MODEL), lambda i: (0, 0, 0)))
        args.append(jnp.stack(next_g)[:, None, :])
    out_specs = [row(D_MODEL)] * (1 + n_next)
    out_shape = [jax.ShapeDtypeStruct((M, D_MODEL), F32)] + [jax.ShapeDtypeStruct((M, D_MODEL), BF16)] * n_next
    res = pl.pallas_call(
        functools.partial(_out_kernel, n_next=n_next), grid=(M // tm,),
        in_specs=in_specs, out_specs=out_specs, out_shape=out_shape,
        scratch_shapes=[pltpu.VMEM((tm, MEM_WIDTH), BF16)],
        compiler_params=_params(1), name="mix_out",
    )(*args)
    return res[0], res[1:]


def kernel(x, mem, positions, pre_norm_g, post_norm_g, mem_norm_g, kv_norm_g,
           w_in_a, w_in_b, w_kv_b, sinks_b, w_mem_kv, w_out):
    B, S, D = x.shape
    Mm = mem.shape[1]
    M = B * S
    w_out = w_out.astype(BF16)

    cos_r, sin_r, cos_k, sin_k, cos_q, sin_q = _rotary_tables(positions.astype(F32).reshape(M, 1))
    mkv = _mem_kv(_rmsnorm_multi(mem.reshape(B * Mm, D), mem_norm_g), w_mem_kv)
    h = x.reshape(M, D)
    y = _rmsnorm_multi(h, pre_norm_g[:1]).reshape(M, D)
    k2 = vt = None
    for l in range(DEPTH):
        if l < N_A:
            z_plain, z_silu = (_in_proj(y, w_in_a, l, kind, tiles) for kind, tiles in TILES_A)
            o_main = _retention(y, w_in_a, l, cos_r, sin_r, B)
        else:
            z_plain, z_silu = (_in_proj(y, w_in_b, l - N_A, kind, tiles) for kind, tiles in TILES_B)
            o_main = _swa(y, w_in_b, l - N_A, cos_q, sin_q, k2, vt, sinks_b[l - N_A], B)
        next_g = [pre_norm_g[l + 1]] if l < DEPTH - 1 else []
        if l == N_A - 1:
            next_g.append(kv_norm_g)
        h, ys = _mix_out(o_main, z_plain, z_silu, mkv, w_out, l, h, post_norm_g[l], next_g, B)
        if ys:
            y = ys[0]
        if l == N_A - 1:
            k2, vt = _shared_kv(ys[1], w_kv_b, cos_k, sin_k)
    return h.reshape(B, S, D)
```

```python
import functools

import jax
import jax.numpy as jnp
from jax import lax
from jax.experimental import pallas as pl
from jax.experimental.pallas import tpu as pltpu

F32 = jnp.float32
BF16 = jnp.bfloat16

D_MODEL = 2048
DEPTH = 4
N_A = DEPTH // 2
RET_DIM = 256
RET_HEADS = D_MODEL // RET_DIM
RET_WIDTH = RET_HEADS * RET_DIM
RET_CHUNK = 256
RET_THETA = 10000.0
SWA_DIM = 64
SWA_HEADS = D_MODEL // SWA_DIM
SWA_KV_HEADS = SWA_HEADS // 8
SWA_WIDTH = SWA_HEADS * SWA_DIM
SWA_KV_WIDTH = SWA_KV_HEADS * SWA_DIM
WINDOW = 128
ROPE_THETA = 500000.0
ROT_DIM = SWA_DIM // 4
MEM_HEADS = 4
MEM_DIM = D_MODEL // 8
MEM_WIDTH = MEM_HEADS * MEM_DIM
EPS = 1e-6
MASK_VALUE = -1e30
LOG2_E = 1.4426950408889634

LANES = 128
VMEM_LIMIT = 56 * 1024 * 1024

NT_DIMS = (((1,), (1,)), ((), ()))
TN_DIMS = (((0,), (0,)), ((), ()))

MEM_GROUP_A = 4 * RET_WIDTH // MEM_WIDTH
MEM_GROUP_B = 2 * SWA_WIDTH // MEM_WIDTH


def _params(n_axes):
    return pltpu.CompilerParams(dimension_semantics=("arbitrary",) * n_axes,
                                vmem_limit_bytes=VMEM_LIMIT)


def _silu(g):
    return g * (1.0 / (1.0 + jnp.exp(-g)))


def _rms(x):
    return x * lax.rsqrt(jnp.mean(x * x, axis=-1, keepdims=True) + EPS)


def _tables_kernel(pos_ref, inv_r_ref, inv_s_ref, sgn_ref, cr_ref, sr_ref, ck_ref, sk_ref, cq_ref, sq_ref):
    pos = pos_ref[...]
    ang_r = pos * inv_r_ref[...]
    cr_ref[...] = jnp.cos(ang_r)
    sr_ref[...] = jnp.sin(ang_r)
    ang_s = pos * inv_s_ref[...]
    cos_s = jnp.cos(ang_s)
    sin_s = jnp.sin(ang_s) * sgn_ref[...]
    ck_ref[...] = cos_s
    sk_ref[...] = sin_s
    cq_ref[...] = cos_s * (LOG2_E * SWA_DIM ** -0.5)
    sq_ref[...] = sin_s * (LOG2_E * SWA_DIM ** -0.5)


def _rotary_tables(pos, tm=1024):
    M = pos.shape[0]
    tm = min(tm, M)
    inv_r = RET_THETA ** (-jnp.arange(0, RET_DIM, 2, dtype=F32) / RET_DIM)
    inv_h = ROPE_THETA ** (-jnp.arange(0, ROT_DIM, 2, dtype=F32) / ROT_DIM)
    half = ROT_DIM // 2
    head = jnp.concatenate([inv_h, inv_h, jnp.zeros((SWA_DIM - ROT_DIM,), F32)])
    sgn_h = jnp.concatenate([-jnp.ones((half,), F32), jnp.ones((half,), F32),
                             jnp.zeros((SWA_DIM - ROT_DIM,), F32)])
    inv_s = jnp.tile(head, LANES // SWA_DIM)
    sgn = jnp.tile(sgn_h, LANES // SWA_DIM)
    row = pl.BlockSpec((1, LANES), lambda i: (0, 0))
    tab = pl.BlockSpec((tm, LANES), lambda i: (i, 0))
    shape = jax.ShapeDtypeStruct((M, LANES), F32)
    return pl.pallas_call(
        _tables_kernel, grid=(M // tm,),
        in_specs=[pl.BlockSpec((tm, 1), lambda i: (i, 0)), row, row, row],
        out_specs=[tab] * 6, out_shape=[shape] * 6,
        compiler_params=_params(1), name="rotary_tables",
    )(pos, inv_r[None], inv_s[None], sgn[None])


def _norm_kernel(x_ref, g_ref, o_ref):
    o_ref[...] = (_rms(x_ref[...]) * g_ref[...]).astype(o_ref.dtype)


def _rmsnorm_multi(x, gains, tm=512):
    M, D = x.shape
    G = gains.shape[0]
    tm = min(tm, M)
    return pl.pallas_call(
        _norm_kernel, grid=(M // tm, G),
        in_specs=[pl.BlockSpec((tm, D), lambda i, g: (i, 0)),
                  pl.BlockSpec((None, 1, D), lambda i, g: (g, 0, 0))],
        out_specs=pl.BlockSpec((None, tm, D), lambda i, g: (g, i, 0)),
        out_shape=jax.ShapeDtypeStruct((G, M, D), BF16),
        compiler_params=_params(2), name="rmsnorm",
    )(x, gains[:, None, :])


def _mem_kv_kernel(a_ref, w_ref, o_ref):
    acc = jnp.dot(a_ref[...], w_ref[...].astype(BF16), preferred_element_type=F32)
    scale = jnp.where(pl.program_id(1) * 2 < pl.num_programs(1), MEM_DIM ** -0.5, 1.0)
    o_ref[...] = (acc * scale).astype(o_ref.dtype)


def _mem_kv(mem_n, w, tn=512):
    L, R, D = mem_n.shape
    N = w.shape[-1]
    return pl.pallas_call(
        _mem_kv_kernel, grid=(L, N // tn),
        in_specs=[pl.BlockSpec((None, R, D), lambda l, j: (l, 0, 0)),
                  pl.BlockSpec((None, D, tn), lambda l, j: (l, 0, j))],
        out_specs=pl.BlockSpec((None, R, tn), lambda l, j: (l, 0, j)),
        out_shape=jax.ShapeDtypeStruct((L, R, N), BF16),
        compiler_params=_params(2), name="mem_kv",
    )(mem_n, w)


def _swa_rope(x, cos, sin_signed):
    lane = lax.broadcasted_iota(jnp.int32, x.shape, 1)
    first_half = (lane & (SWA_DIM - 1)) < (ROT_DIM // 2)
    partner = jnp.where(first_half, pltpu.roll(x, LANES - ROT_DIM // 2, 1), pltpu.roll(x, ROT_DIM // 2, 1))
    return x * cos + partner * sin_signed


IN_CHUNK = 256


def _mem_mixer_kernel(y_ref, wq_ref, wg_ref, mk_ref, mv_ref, o_ref, wb_ref):
    D = MEM_DIM

    @pl.when(pl.program_id(1) == 0)
    def _cast_weights():
        wb_ref[:, :D] = wq_ref[...].astype(BF16)
        wb_ref[:, D:] = wg_ref[...].astype(BF16)

    def project(c):
        z = jnp.dot(y_ref[pl.ds(c * IN_CHUNK, IN_CHUNK), :], wb_ref[...], preferred_element_type=F32)
        return z[:, :D].astype(BF16), _silu(z[:, D:])

    n_chunks = y_ref.shape[0] // IN_CHUNK
    nxt = project(0)
    for c in range(n_chunks):
        q, sg = nxt
        s = lax.dot_general(q, mk_ref[...], NT_DIMS, preferred_element_type=F32)
        if c + 1 < n_chunks:
            nxt = project(c + 1)
        e = jnp.exp(s - jnp.max(s, axis=-1, keepdims=True))
        p = (e / jnp.sum(e, axis=-1, keepdims=True)).astype(BF16)
        o = jnp.dot(p, mv_ref[...], preferred_element_type=F32)
        o_ref[pl.ds(c * IN_CHUNK, IN_CHUNK), :] = (o * sg).astype(o_ref.dtype)


def _mem_mixer(y, w, layer, q_group, mkv, mkv_layer, batch, tm=2048):
    M, K = y.shape
    S = M // batch
    tm = min(tm, S)
    tiles_per_seq = S // tm
    Mm = mkv.shape[1] // batch
    H = MEM_HEADS

    def wcol(group):
        return pl.BlockSpec((None, K, MEM_DIM), lambda m, i: (layer, 0, group * H + m))

    def mem(off):
        return pl.BlockSpec((None, Mm, MEM_DIM), lambda m, i: (mkv_layer, i // tiles_per_seq, off * H + m))

    return pl.pallas_call(
        _mem_mixer_kernel, grid=(H, M // tm),
        in_specs=[pl.BlockSpec((tm, K), lambda m, i: (i, 0)), wcol(q_group), wcol(q_group + 1), mem(0), mem(1)],
        out_specs=pl.BlockSpec((tm, MEM_DIM), lambda m, i: (i, m)),
        out_shape=jax.ShapeDtypeStruct((M, MEM_WIDTH), BF16),
        scratch_shapes=[pltpu.VMEM((K, 2 * MEM_DIM), BF16)],
        compiler_params=_params(2), name="mem_mixer",
    )(y, w, w, mkv, mkv)


def _side_cast_specs(w_out, layer, n_steps, steps_per_row):
    rows, cols = w_out.shape[1:]
    slab = rows // n_steps
    assert slab * n_steps == rows and slab % 16 == 0, (rows, n_steps)
    return (pl.BlockSpec((None, slab, cols), lambda a, b: (layer, a * steps_per_row + b, 0)),
            pl.BlockSpec((slab, cols), lambda a, b: (a * steps_per_row + b, 0)),
            jax.ShapeDtypeStruct((rows, cols), BF16))


def _retention_kernel(y_ref, wq_ref, wk_ref, wv_ref, wg_ref, cos_ref, sin_ref, wo_ref, o_ref, wo_bf16_ref,
                      wb_ref, r_ref, dec_ref, xi_ref, zeta_ref, gc_ref, *, tiles_per_seq):
    C = RET_CHUNK
    D = RET_DIM
    h = pl.program_id(0)
    i = pl.program_id(1)
    n_chunks = y_ref.shape[0] // C
    k_scale = RET_DIM ** -0.5
    wo_bf16_ref[...] = wo_ref[...].astype(BF16)

    @pl.when(i == 0)
    def _init():
        def log_gamma(shape):
            return jnp.log(1.0 - jnp.exp2(-5.0 - jnp.full(shape, h, jnp.int32).astype(F32)))

        for c, w_ref in enumerate((wq_ref, wk_ref, wv_ref, wg_ref)):
            wb_ref[:, c * D:(c + 1) * D] = w_ref[...].astype(BF16)
        n = lax.broadcasted_iota(jnp.int32, (C, RET_DIM), 0).astype(F32)
        xi_ref[...] = jnp.exp(log_gamma((C, RET_DIM)) * (n + 1.0))
        zeta_ref[...] = jnp.exp(log_gamma((C, RET_DIM)) * (C - 1.0 - n)) * k_scale
        gc_ref[...] = jnp.exp(log_gamma(gc_ref.shape) * float(C))
        diff = (lax.broadcasted_iota(jnp.int32, (C, C), 0)
                - lax.broadcasted_iota(jnp.int32, (C, C), 1)).astype(F32)
        dec_ref[...] = jnp.where(diff >= 0, jnp.exp(log_gamma((C, C)) * jnp.maximum(diff, 0.0)) * k_scale, 0.0)

    @pl.when(i % tiles_per_seq == 0)
    def _reset_state():
        r_ref[...] = jnp.zeros_like(r_ref)

    half = D // 2

    def rope(x, cos, sin):
        x1, x2 = x[:, :half], x[:, half:]
        return jnp.concatenate([x1 * cos - x2 * sin, x2 * cos + x1 * sin], axis=-1)

    def project(c):
        rows = pl.ds(c * C, C)
        z = jnp.dot(y_ref[rows, :], wb_ref[...], preferred_element_type=F32)
        cos, sin = cos_ref[rows, :], sin_ref[rows, :]
        k = rope(z[:, D:2 * D], cos, sin)
        return (rope(z[:, :D], cos, sin).astype(BF16), k.astype(BF16), (k * zeta_ref[...]).astype(BF16),
                z[:, 2 * D:3 * D].astype(BF16), _silu(z[:, 3 * D:]).astype(BF16))

    r = r_ref[...]
    nxt = project(0)
    for c in range(n_chunks):
        q, k, kz, v, sg = nxt
        if c + 1 < n_chunks:
            nxt = project(c + 1)
        inner = lax.dot_general(q, k, NT_DIMS, preferred_element_type=F32)
        update = lax.dot_general(kz, v, TN_DIMS, preferred_element_type=F32)
        cross = jnp.dot(q, r.astype(BF16), preferred_element_type=F32)
        o = jnp.dot((inner * dec_ref[...]).astype(BF16), v, preferred_element_type=F32) + xi_ref[...] * cross
        r = r * gc_ref[0:1, :] + update
        o_ref[pl.ds(c * C, C), :] = (_rms(o) * sg.astype(F32)).astype(o_ref.dtype)
    r_ref[...] = r


def _retention(y, w, layer, cos, sin, w_out, batch, tm=2048):
    M, K = y.shape
    S = M // batch
    tm = min(tm, S)
    H = RET_HEADS
    wo_in, wo_out, wo_shape = _side_cast_specs(w_out, layer, H * (M // tm), M // tm)

    def wcol(group):
        return pl.BlockSpec((None, K, RET_DIM), lambda h, i: (layer, 0, group * H + h))

    tab = pl.BlockSpec((tm, LANES), lambda h, i: (i, 0))
    return pl.pallas_call(
        functools.partial(_retention_kernel, tiles_per_seq=S // tm),
        grid=(H, M // tm),
        in_specs=[pl.BlockSpec((tm, K), lambda h, i: (i, 0)), wcol(0), wcol(1), wcol(2), wcol(3), tab, tab, wo_in],
        out_specs=[pl.BlockSpec((tm, RET_DIM), lambda h, i: (i, h)), wo_out],
        out_shape=[jax.ShapeDtypeStruct((M, RET_WIDTH), BF16), wo_shape],
        scratch_shapes=[pltpu.VMEM((K, 4 * RET_DIM), BF16),
                        pltpu.VMEM((RET_DIM, RET_DIM), F32),
                        pltpu.VMEM((RET_CHUNK, RET_CHUNK), F32),
                        pltpu.VMEM((RET_CHUNK, RET_DIM), F32),
                        pltpu.VMEM((RET_CHUNK, RET_DIM), F32),
                        pltpu.VMEM((8, RET_DIM), F32)],
        compiler_params=_params(2), name="retention",
    )(y, w, w, w, w, cos, sin, w_out)


def _kv_kernel(y_ref, w_ref, cos_ref, sin_ref, k_ref, vt_ref, wb_ref):
    @pl.when(pl.program_id(0) == 0)
    def _cast_weight():
        wb_ref[...] = w_ref[...].astype(BF16)

    low = lax.broadcasted_iota(jnp.int32, (IN_CHUNK, LANES), 1) < SWA_DIM
    for r in range(0, y_ref.shape[0], IN_CHUNK):
        rows = pl.ds(r, IN_CHUNK)
        kv = jnp.dot(y_ref[rows, :], wb_ref[...], preferred_element_type=F32)
        cos, sin = cos_ref[rows, :], sin_ref[rows, :]
        for blk in range(SWA_KV_WIDTH // LANES):
            k = _swa_rope(kv[:, blk * LANES:(blk + 1) * LANES], cos, sin)
            swapped = pltpu.roll(k, SWA_DIM, 1)
            tiles = (jnp.where(low, k, 0.0), jnp.where(low, 0.0, swapped),
                     jnp.where(low, swapped, 0.0), jnp.where(low, 0.0, k))
            for i, t in enumerate(tiles):
                k_ref[rows, (4 * blk + i) * LANES:(4 * blk + i + 1) * LANES] = t.astype(k_ref.dtype)
        vt_ref[:, r:r + IN_CHUNK] = kv[:, SWA_KV_WIDTH:].T.astype(vt_ref.dtype)


SWA_K_COLS = 2 * LANES * SWA_KV_HEADS


def _shared_kv(y, w, cos, sin, tm=1024):
    M, D = y.shape
    tm = min(tm, M)
    tab = pl.BlockSpec((tm, LANES), lambda i: (i, 0))
    return pl.pallas_call(
        _kv_kernel, grid=(M // tm,),
        in_specs=[pl.BlockSpec((tm, D), lambda i: (i, 0)),
                  pl.BlockSpec((D, 2 * SWA_KV_WIDTH), lambda i: (0, 0)), tab, tab],
        out_specs=[pl.BlockSpec((tm, SWA_K_COLS), lambda i: (i, 0)),
                   pl.BlockSpec((SWA_KV_WIDTH, tm), lambda i: (0, i))],
        out_shape=[jax.ShapeDtypeStruct((M, SWA_K_COLS), BF16),
                   jax.ShapeDtypeStruct((SWA_KV_WIDTH, M), BF16)],
        scratch_shapes=[pltpu.VMEM((D, 2 * SWA_KV_WIDTH), BF16)],
        compiler_params=_params(1), name="shared_kv",
    )(y, w, cos, sin)


SWA_ONES_ROWS = 16


SWA_GROUP = SWA_WIDTH // SWA_KV_HEADS


def _swa_kernel(sink_ref, y_ref, wq_ref, wg_ref, cos_ref, sin_ref, kp_ref, kc_ref, vtp_ref, vtc_ref, wo_ref,
                o_ref, wo_bf16_ref, wb_ref, *, tiles_per_seq):
    wo_bf16_ref[...] = wo_ref[...].astype(BF16)
    W = WINDOW
    n_pairs = SWA_GROUP // LANES
    QN = n_pairs * W
    g = pl.program_id(0)
    i = pl.program_id(1)
    n_chunks = y_ref.shape[0] // IN_CHUNK
    blocks_per_chunk = IN_CHUNK // W

    @pl.when(i == 0)
    def _cast_weights():
        wb_ref[:, :SWA_GROUP] = wq_ref[...].astype(BF16)
        wb_ref[:, SWA_GROUP:] = wg_ref[...].astype(BF16)

    jrow = lax.broadcasted_iota(jnp.int32, (W, QN), 0)
    icol = lax.broadcasted_iota(jnp.int32, (W, QN), 1)
    upper = jrow > (icol & (W - 1))
    upper_bf16 = jnp.where(upper, 1.0, 0.0).astype(BF16)
    pair_of_lane = lax.broadcasted_iota(jnp.int32, (1, QN), 1) // W
    ones_rows = jnp.ones((SWA_ONES_ROWS, W), BF16)
    no_prev_bias = jnp.where(i % tiles_per_seq == 0, MASK_VALUE, 0.0)

    def project(c):
        rows = pl.ds(c * IN_CHUNK, IN_CHUNK)
        z = jnp.dot(y_ref[rows, :], wb_ref[...], preferred_element_type=F32)
        cos, sin = cos_ref[rows, :], sin_ref[rows, :]
        q = [_swa_rope(z[:, j * LANES:(j + 1) * LANES], cos, sin).astype(BF16) for j in range(n_pairs)]
        sg = _silu(z[:, SWA_GROUP:]).astype(BF16)
        return [(jnp.concatenate([qj[b * W:(b + 1) * W] for qj in q], axis=0), sg[b * W:(b + 1) * W])
                for b in range(blocks_per_chunk)]

    def keys_of(t, parity):
        kcols = slice(parity * LANES, (parity + 1) * LANES)
        prev = kp_ref[:, kcols] if t == 0 else kc_ref[pl.ds((t - 1) * W, W), kcols]
        return [prev, kc_ref[pl.ds(t * W, W), kcols]]

    def scores(t, q):
        keys = jnp.concatenate(keys_of(t, 0) + keys_of(t, 1), axis=0)
        return lax.dot_general(keys, q, NT_DIMS, preferred_element_type=F32)

    def softmax(t, st):
        out = []
        for parity in range(2):
            sp = st[parity * 2 * W:(parity + 1) * 2 * W]
            s_prev = sp[:W] + no_prev_bias if t == 0 else sp[:W]
            s = jnp.where(upper, s_prev, sp[W:])
            sink = jnp.full((1, QN), sink_ref[2 * n_pairs * g + 2 * (n_pairs - 1) + parity], F32)
            for j in range(n_pairs - 1):
                sink = jnp.where(pair_of_lane == j, sink_ref[2 * n_pairs * g + 2 * j + parity], sink)
            sink = sink * LOG2_E
            m = jnp.maximum(jnp.max(s, axis=0, keepdims=True), sink)
            e = jnp.exp2(s - m).astype(BF16)
            e_prev = e * upper_bf16
            out.append((jnp.concatenate([e_prev, e - e_prev], axis=0), jnp.exp2(sink - m)))
        return out

    def finish(t, probs, sg):
        vt_prev = vtp_ref[...] if t == 0 else vtc_ref[:, (t - 1) * W:t * W]
        vt = jnp.concatenate([jnp.concatenate([vt_prev, ones_rows], axis=0),
                              jnp.concatenate([vtc_ref[:, t * W:(t + 1) * W], ones_rows], axis=0)], axis=1)
        ot = []
        for e, sink_term in probs:
            acc = jnp.dot(vt, e, preferred_element_type=F32)
            ot.append(acc[:SWA_DIM] / (acc[SWA_DIM:SWA_DIM + 1] + sink_term))
        for j in range(n_pairs):
            cols = slice(j * LANES, (j + 1) * LANES)
            o = jnp.concatenate([ot[0][:, j * W:(j + 1) * W], ot[1][:, j * W:(j + 1) * W]], axis=0).T
            o_ref[pl.ds(t * W, W), cols] = (o * sg[:, cols].astype(F32)).astype(o_ref.dtype)

    blocks = project(0)
    for c in range(n_chunks):
        ts = [c * blocks_per_chunk + b for b in range(blocks_per_chunk)]
        sts = [scores(t, q) for t, (q, _) in zip(ts, blocks)]
        nxt = project(c + 1) if c + 1 < n_chunks else None
        probs = [softmax(t, st) for t, st in zip(ts, sts)]
        for t, p, (_, sg) in zip(ts, probs, blocks):
            finish(t, p, sg)
        blocks = nxt


def _swa(y, w, layer, cos, sin, k2, vt, sinks, w_out, out_layer, batch, tm=2048):
    M, K = y.shape
    S = M // batch
    tm = min(tm, S)
    n_groups = SWA_KV_HEADS
    blocks_per_tile = tm // WINDOW
    wo_in, wo_out, wo_shape = _side_cast_specs(w_out, out_layer, n_groups * (M // tm), M // tm)

    def before(i):
        return jnp.maximum(i * blocks_per_tile - 1, 0)

    tab = pl.BlockSpec((tm, LANES), lambda g, i: (i, 0))
    return pl.pallas_call(
        functools.partial(_swa_kernel, tiles_per_seq=S // tm),
        grid=(n_groups, M // tm),
        in_specs=[pl.BlockSpec(memory_space=pltpu.SMEM),
                  pl.BlockSpec((tm, K), lambda g, i: (i, 0)),
                  pl.BlockSpec((None, K, SWA_GROUP), lambda g, i: (layer, 0, g)),
                  pl.BlockSpec((None, K, SWA_GROUP), lambda g, i: (layer, 0, n_groups + g)),
                  tab, tab,
                  pl.BlockSpec((WINDOW, 2 * LANES), lambda g, i: (before(i), g)),
                  pl.BlockSpec((tm, 2 * LANES), lambda g, i: (i, g)),
                  pl.BlockSpec((SWA_DIM, WINDOW), lambda g, i: (g, before(i))),
                  pl.BlockSpec((SWA_DIM, tm), lambda g, i: (g, i)), wo_in],
        out_specs=[pl.BlockSpec((tm, SWA_GROUP), lambda g, i: (i, g)), wo_out],
        out_shape=[jax.ShapeDtypeStruct((M, SWA_WIDTH), BF16), wo_shape],
        scratch_shapes=[pltpu.VMEM((K, 2 * SWA_GROUP), BF16)],
        compiler_params=_params(2), name="swa",
    )(sinks, y, w, w, cos, sin, k2, k2, vt, vt, w_out)


OUT_CHUNK = 128


def _out_kernel(a1_ref, a2_ref, w1_ref, w2_ref, h_ref, pg_ref, *rest, n_next):
    ng_ref = rest[0] if n_next else None
    h_out_ref = rest[1] if n_next else rest[0]
    y_refs = rest[2:2 + n_next]
    for r in range(0, a1_ref.shape[0], OUT_CHUNK):
        rows = pl.ds(r, OUT_CHUNK)
        acc = (jnp.dot(a1_ref[rows, :], w1_ref[...], preferred_element_type=F32)
               + jnp.dot(a2_ref[rows, :], w2_ref[...], preferred_element_type=F32))
        h = h_ref[rows, :] + _rms(acc) * pg_ref[...]
        h_out_ref[rows, :] = h
        if n_next:
            hn = _rms(h)
            for g in range(n_next):
                y_refs[g][rows, :] = (hn * ng_ref[g]).astype(y_refs[g].dtype)


def _mix_out(o_main, o_mem, w_out, h, post_g, next_g, tm=512):
    M = h.shape[0]
    tm = min(tm, M)
    n_next = len(next_g)
    k1, k2 = o_main.shape[1], o_mem.shape[1]
    assert k1 % k2 == 0

    def row(width):
        return pl.BlockSpec((tm, width), lambda i: (i, 0))

    once = pl.Buffered(1)
    in_specs = [row(k1), row(k2),
                pl.BlockSpec((k1, D_MODEL), lambda i: (0, 0), pipeline_mode=once),
                pl.BlockSpec((k2, D_MODEL), lambda i: (k1 // k2, 0), pipeline_mode=once),
                row(D_MODEL),
                pl.BlockSpec((1, D_MODEL), lambda i: (0, 0))]
    args = [o_main, o_mem, w_out, w_out, h, post_g[None]]
    if n_next:
        in_specs.append(pl.BlockSpec((n_next, 1, D_MODEL), lambda i: (0, 0, 0)))
        args.append(jnp.stack(next_g)[:, None, :])
    out_specs = [row(D_MODEL)] * (1 + n_next)
    out_shape = [jax.ShapeDtypeStruct((M, D_MODEL), F32)] + [jax.ShapeDtypeStruct((M, D_MODEL), BF16)] * n_next
    res = pl.pallas_call(
        functools.partial(_out_kernel, n_next=n_next), grid=(M // tm,),
        in_specs=in_specs, out_specs=out_specs, out_shape=out_shape,
        compiler_params=_params(1), name="mix_out",
    )(*args)
    return res[0], res[1:]


def kernel(x, mem, positions, pre_norm_g, post_norm_g, mem_norm_g, kv_norm_g,
           w_in_a, w_in_b, w_kv_b, sinks_b, w_mem_kv, w_out):
    B, S, D = x.shape
    Mm = mem.shape[1]
    M = B * S
    cos_r, sin_r, cos_k, sin_k, cos_q, sin_q = _rotary_tables(positions.astype(F32).reshape(M, 1))
    mkv = _mem_kv(_rmsnorm_multi(mem.reshape(B * Mm, D), mem_norm_g), w_mem_kv)
    h = x.reshape(M, D)
    y = _rmsnorm_multi(h, pre_norm_g[:1]).reshape(M, D)
    k2 = vt = None
    for l in range(DEPTH):
        if l < N_A:
            o_mem = _mem_mixer(y, w_in_a, l, MEM_GROUP_A, mkv, l, B)
            o_main, w_out_l = _retention(y, w_in_a, l, cos_r, sin_r, w_out, B)
        else:
            o_mem = _mem_mixer(y, w_in_b, l - N_A, MEM_GROUP_B, mkv, l, B)
            o_main, w_out_l = _swa(y, w_in_b, l - N_A, cos_q, sin_q, k2, vt, sinks_b[l - N_A], w_out, l, B)
        next_g = [pre_norm_g[l + 1]] if l < DEPTH - 1 else []
        if l == N_A - 1:
            next_g.append(kv_norm_g)
        h, ys = _mix_out(o_main, o_mem, w_out_l, h, post_norm_g[l], next_g)
        if ys:
            y = ys[0]
        if l == N_A - 1:
            k2, vt = _shared_kv(ys[1], w_kv_b, cos_k, sin_k)
    return h.reshape(B, S, D)
```

```python
import functools

import jax
import jax.numpy as jnp
from jax import lax
from jax.experimental import pallas as pl
from jax.experimental.pallas import tpu as pltpu

F32 = jnp.float32
BF16 = jnp.bfloat16

D_MODEL = 2048
DEPTH = 4
N_A = DEPTH // 2
RET_DIM = 256
RET_HEADS = D_MODEL // RET_DIM
RET_WIDTH = RET_HEADS * RET_DIM
RET_CHUNK = 256
RET_THETA = 10000.0
SWA_DIM = 64
SWA_HEADS = D_MODEL // SWA_DIM
SWA_KV_HEADS = SWA_HEADS // 8
SWA_WIDTH = SWA_HEADS * SWA_DIM
SWA_KV_WIDTH = SWA_KV_HEADS * SWA_DIM
WINDOW = 128
ROPE_THETA = 500000.0
ROT_DIM = SWA_DIM // 4
MEM_HEADS = 4
MEM_DIM = D_MODEL // 8
MEM_WIDTH = MEM_HEADS * MEM_DIM
EPS = 1e-6
MASK_VALUE = -1e30
LOG2_E = 1.4426950408889634

LANES = 128
VMEM_LIMIT = 56 * 1024 * 1024

NT_DIMS = (((1,), (1,)), ((), ()))
TN_DIMS = (((0,), (0,)), ((), ()))

MEM_GROUP_A = 4 * RET_WIDTH // MEM_WIDTH
MEM_GROUP_B = 2 * SWA_WIDTH // MEM_WIDTH


def _params(n_axes):
    return pltpu.CompilerParams(dimension_semantics=("arbitrary",) * n_axes,
                                vmem_limit_bytes=VMEM_LIMIT)


def _silu(g):
    return g * (1.0 / (1.0 + jnp.exp(-g)))


def _rms(x):
    return x * lax.rsqrt(jnp.mean(x * x, axis=-1, keepdims=True) + EPS)


def _prologue_kernel(x_ref, g_ref, pos_ref, inv_r_ref, inv_s_ref, sgn_ref,
                     y_ref, cr_ref, sr_ref, ck_ref, sk_ref, cq_ref, sq_ref):
    y_ref[...] = (_rms(x_ref[...]) * g_ref[...]).astype(y_ref.dtype)
    pos = pos_ref[...]
    ang_r = pos * inv_r_ref[...]
    cr_ref[...] = jnp.cos(ang_r)
    sr_ref[...] = jnp.sin(ang_r)
    ang_s = pos * inv_s_ref[...]
    cos_s = jnp.cos(ang_s)
    sin_s = jnp.sin(ang_s) * sgn_ref[...]
    ck_ref[...] = cos_s
    sk_ref[...] = sin_s
    cq_ref[...] = cos_s * (LOG2_E * SWA_DIM ** -0.5)
    sq_ref[...] = sin_s * (LOG2_E * SWA_DIM ** -0.5)


def _prologue(x, gain, pos, tm=512):
    M, D = x.shape
    tm = min(tm, M)
    inv_r = RET_THETA ** (-jnp.arange(0, RET_DIM, 2, dtype=F32) / RET_DIM)
    inv_h = ROPE_THETA ** (-jnp.arange(0, ROT_DIM, 2, dtype=F32) / ROT_DIM)
    half = ROT_DIM // 2
    head = jnp.concatenate([inv_h, inv_h, jnp.zeros((SWA_DIM - ROT_DIM,), F32)])
    sgn_h = jnp.concatenate([-jnp.ones((half,), F32), jnp.ones((half,), F32),
                             jnp.zeros((SWA_DIM - ROT_DIM,), F32)])
    inv_s = jnp.tile(head, LANES // SWA_DIM)
    sgn = jnp.tile(sgn_h, LANES // SWA_DIM)
    row = pl.BlockSpec((1, LANES), lambda i: (0, 0))
    tab = pl.BlockSpec((tm, LANES), lambda i: (i, 0))
    shape = jax.ShapeDtypeStruct((M, LANES), F32)
    wide = pl.BlockSpec((tm, D), lambda i: (i, 0))
    return pl.pallas_call(
        _prologue_kernel, grid=(M // tm,),
        in_specs=[wide, pl.BlockSpec((1, D), lambda i: (0, 0)), pl.BlockSpec((tm, 1), lambda i: (i, 0)),
                  row, row, row],
        out_specs=[wide] + [tab] * 6,
        out_shape=[jax.ShapeDtypeStruct((M, D), BF16)] + [shape] * 6,
        compiler_params=_params(1), name="prologue",
    )(x, gain[None], pos, inv_r[None], inv_s[None], sgn[None])


def _mem_kv_kernel(mem_ref, g_ref, w_ref, o_ref, a_ref):
    @pl.when(pl.program_id(1) == 0)
    def _norm():
        a_ref[...] = (_rms(mem_ref[...]) * g_ref[...]).astype(a_ref.dtype)

    acc = jnp.dot(a_ref[...], w_ref[...].astype(BF16), preferred_element_type=F32)
    scale = jnp.where(pl.program_id(1) * 2 < pl.num_programs(1), MEM_DIM ** -0.5, 1.0)
    o_ref[...] = (acc * scale).astype(o_ref.dtype)


def _mem_kv(mem, gains, w, tn=512):
    R, D = mem.shape
    L, _, N = w.shape
    return pl.pallas_call(
        _mem_kv_kernel, grid=(L, N // tn),
        in_specs=[pl.BlockSpec((R, D), lambda l, j: (0, 0)),
                  pl.BlockSpec((None, 1, D), lambda l, j: (l, 0, 0)),
                  pl.BlockSpec((None, D, tn), lambda l, j: (l, 0, j))],
        out_specs=pl.BlockSpec((None, R, tn), lambda l, j: (l, 0, j)),
        out_shape=jax.ShapeDtypeStruct((L, R, N), BF16),
        scratch_shapes=[pltpu.VMEM((R, D), BF16)],
        compiler_params=_params(2), name="mem_kv",
    )(mem, gains[:, None, :], w)


def _swa_rope(x, cos, sin_signed):
    lane = lax.broadcasted_iota(jnp.int32, x.shape, 1)
    first_half = (lane & (SWA_DIM - 1)) < (ROT_DIM // 2)
    partner = jnp.where(first_half, pltpu.roll(x, LANES - ROT_DIM // 2, 1), pltpu.roll(x, ROT_DIM // 2, 1))
    return x * cos + partner * sin_signed


IN_CHUNK = 256


def _mem_mixer_kernel(y_ref, wq_ref, wg_ref, mk_ref, mv_ref, o_ref, wb_ref):
    D = MEM_DIM

    @pl.when(pl.program_id(1) == 0)
    def _cast_weights():
        wb_ref[:, :D] = wq_ref[...].astype(BF16)
        wb_ref[:, D:] = wg_ref[...].astype(BF16)

    def project(c):
        z = jnp.dot(y_ref[pl.ds(c * IN_CHUNK, IN_CHUNK), :], wb_ref[...], preferred_element_type=F32)
        return z[:, :D].astype(BF16), _silu(z[:, D:])

    n_chunks = y_ref.shape[0] // IN_CHUNK
    nxt = project(0)
    for c in range(n_chunks):
        q, sg = nxt
        s = lax.dot_general(q, mk_ref[...], NT_DIMS, preferred_element_type=F32)
        if c + 1 < n_chunks:
            nxt = project(c + 1)
        e = jnp.exp(s - jnp.max(s, axis=-1, keepdims=True))
        p = (e / jnp.sum(e, axis=-1, keepdims=True)).astype(BF16)
        o = jnp.dot(p, mv_ref[...], preferred_element_type=F32)
        o_ref[pl.ds(c * IN_CHUNK, IN_CHUNK), :] = (o * sg).astype(o_ref.dtype)


def _mem_mixer(y, w, layer, q_group, mkv, mkv_layer, batch, tm=4096):
    M, K = y.shape
    S = M // batch
    tm = min(tm, S)
    tiles_per_seq = S // tm
    Mm = mkv.shape[1] // batch
    H = MEM_HEADS

    def wcol(group):
        return pl.BlockSpec((None, K, MEM_DIM), lambda m, i: (layer, 0, group * H + m))

    def mem(off):
        return pl.BlockSpec((None, Mm, MEM_DIM), lambda m, i: (mkv_layer, i // tiles_per_seq, off * H + m))

    return pl.pallas_call(
        _mem_mixer_kernel, grid=(H, M // tm),
        in_specs=[pl.BlockSpec((tm, K), lambda m, i: (i, 0)), wcol(q_group), wcol(q_group + 1), mem(0), mem(1)],
        out_specs=pl.BlockSpec((tm, MEM_DIM), lambda m, i: (i, m)),
        out_shape=jax.ShapeDtypeStruct((M, MEM_WIDTH), BF16),
        scratch_shapes=[pltpu.VMEM((K, 2 * MEM_DIM), BF16)],
        compiler_params=_params(2), name="mem_mixer",
    )(y, w, w, mkv, mkv)


def _side_cast_specs(w_out, layer, n_steps, steps_per_row):
    rows, cols = w_out.shape[1:]
    slab = rows // n_steps
    assert slab * n_steps == rows and slab % 16 == 0, (rows, n_steps)
    return (pl.BlockSpec((None, slab, cols), lambda a, b: (layer, a * steps_per_row + b, 0)),
            pl.BlockSpec((slab, cols), lambda a, b: (a * steps_per_row + b, 0)),
            jax.ShapeDtypeStruct((rows, cols), BF16))


def _retention_kernel(y_ref, wq_ref, wk_ref, wv_ref, wg_ref, cos_ref, sin_ref, wo_ref, o_ref, wo_bf16_ref,
                      wb_ref, r_ref, dec_ref, xi_ref, zeta_ref, gc_ref, *, tiles_per_seq):
    C = RET_CHUNK
    D = RET_DIM
    h = pl.program_id(0)
    i = pl.program_id(1)
    n_chunks = y_ref.shape[0] // C
    k_scale = RET_DIM ** -0.5
    wo_bf16_ref[...] = wo_ref[...].astype(BF16)

    @pl.when(i == 0)
    def _init():
        def log_gamma(shape):
            return jnp.log(1.0 - jnp.exp2(-5.0 - jnp.full(shape, h, jnp.int32).astype(F32)))

        for c, w_ref in enumerate((wq_ref, wk_ref, wv_ref, wg_ref)):
            wb_ref[:, c * D:(c + 1) * D] = w_ref[...].astype(BF16)
        n = lax.broadcasted_iota(jnp.int32, (C, RET_DIM), 0).astype(F32)
        xi_ref[...] = jnp.exp(log_gamma((C, RET_DIM)) * (n + 1.0))
        zeta_ref[...] = jnp.exp(log_gamma((C, RET_DIM)) * (C - 1.0 - n)) * k_scale
        gc_ref[...] = jnp.exp(log_gamma(gc_ref.shape) * float(C))
        diff = (lax.broadcasted_iota(jnp.int32, (C, C), 0)
                - lax.broadcasted_iota(jnp.int32, (C, C), 1)).astype(F32)
        dec_ref[...] = jnp.where(diff >= 0, jnp.exp(log_gamma((C, C)) * jnp.maximum(diff, 0.0)) * k_scale, 0.0)

    @pl.when(i % tiles_per_seq == 0)
    def _reset_state():
        r_ref[...] = jnp.zeros_like(r_ref)

    half = D // 2

    def rope(x, cos, sin):
        x1, x2 = x[:, :half], x[:, half:]
        return jnp.concatenate([x1 * cos - x2 * sin, x2 * cos + x1 * sin], axis=-1)

    def project(c):
        rows = pl.ds(c * C, C)
        z = jnp.dot(y_ref[rows, :], wb_ref[...], preferred_element_type=F32)
        cos, sin = cos_ref[rows, :], sin_ref[rows, :]
        k = rope(z[:, D:2 * D], cos, sin)
        return (rope(z[:, :D], cos, sin).astype(BF16), k.astype(BF16), (k * zeta_ref[...]).astype(BF16),
                z[:, 2 * D:3 * D].astype(BF16), _silu(z[:, 3 * D:]).astype(BF16))

    r = r_ref[...]
    nxt = project(0)
    for c in range(n_chunks):
        q, k, kz, v, sg = nxt
        if c + 1 < n_chunks:
            nxt = project(c + 1)
        inner = lax.dot_general(q, k, NT_DIMS, preferred_element_type=F32)
        update = lax.dot_general(kz, v, TN_DIMS, preferred_element_type=F32)
        cross = jnp.dot(q, r.astype(BF16), preferred_element_type=F32)
        o = jnp.dot((inner * dec_ref[...]).astype(BF16), v, preferred_element_type=F32) + xi_ref[...] * cross
        r = r * gc_ref[0:1, :] + update
        o_ref[pl.ds(c * C, C), :] = (_rms(o) * sg.astype(F32)).astype(o_ref.dtype)
    r_ref[...] = r


def _retention(y, w, layer, cos, sin, w_out, batch, tm=2048):
    M, K = y.shape
    S = M // batch
    tm = min(tm, S)
    H = RET_HEADS
    wo_in, wo_out, wo_shape = _side_cast_specs(w_out, layer, H * (M // tm), M // tm)

    def wcol(group):
        return pl.BlockSpec((None, K, RET_DIM), lambda h, i: (layer, 0, group * H + h))

    tab = pl.BlockSpec((tm, LANES), lambda h, i: (i, 0))
    return pl.pallas_call(
        functools.partial(_retention_kernel, tiles_per_seq=S // tm),
        grid=(H, M // tm),
        in_specs=[pl.BlockSpec((tm, K), lambda h, i: (i, 0)), wcol(0), wcol(1), wcol(2), wcol(3), tab, tab, wo_in],
        out_specs=[pl.BlockSpec((tm, RET_DIM), lambda h, i: (i, h)), wo_out],
        out_shape=[jax.ShapeDtypeStruct((M, RET_WIDTH), BF16), wo_shape],
        scratch_shapes=[pltpu.VMEM((K, 4 * RET_DIM), BF16),
                        pltpu.VMEM((RET_DIM, RET_DIM), F32),
                        pltpu.VMEM((RET_CHUNK, RET_CHUNK), F32),
                        pltpu.VMEM((RET_CHUNK, RET_DIM), F32),
                        pltpu.VMEM((RET_CHUNK, RET_DIM), F32),
                        pltpu.VMEM((8, RET_DIM), F32)],
        compiler_params=_params(2), name="retention",
    )(y, w, w, w, w, cos, sin, w_out)


def _kv_kernel(y_ref, w_ref, cos_ref, sin_ref, k_ref, vt_ref, wb_ref):
    @pl.when(pl.program_id(0) == 0)
    def _cast_weight():
        wb_ref[...] = w_ref[...].astype(BF16)

    low = lax.broadcasted_iota(jnp.int32, (IN_CHUNK, LANES), 1) < SWA_DIM
    for r in range(0, y_ref.shape[0], IN_CHUNK):
        rows = pl.ds(r, IN_CHUNK)
        kv = jnp.dot(y_ref[rows, :], wb_ref[...], preferred_element_type=F32)
        cos, sin = cos_ref[rows, :], sin_ref[rows, :]
        for blk in range(SWA_KV_WIDTH // LANES):
            k = _swa_rope(kv[:, blk * LANES:(blk + 1) * LANES], cos, sin)
            swapped = pltpu.roll(k, SWA_DIM, 1)
            tiles = (jnp.where(low, k, 0.0), jnp.where(low, 0.0, swapped),
                     jnp.where(low, swapped, 0.0), jnp.where(low, 0.0, k))
            for i, t in enumerate(tiles):
                k_ref[rows, (4 * blk + i) * LANES:(4 * blk + i + 1) * LANES] = t.astype(k_ref.dtype)
        vt_ref[:, r:r + IN_CHUNK] = kv[:, SWA_KV_WIDTH:].T.astype(vt_ref.dtype)


SWA_K_COLS = 2 * LANES * SWA_KV_HEADS


def _shared_kv(y, w, cos, sin, tm=1024):
    M, D = y.shape
    tm = min(tm, M)
    tab = pl.BlockSpec((tm, LANES), lambda i: (i, 0))
    return pl.pallas_call(
        _kv_kernel, grid=(M // tm,),
        in_specs=[pl.BlockSpec((tm, D), lambda i: (i, 0)),
                  pl.BlockSpec((D, 2 * SWA_KV_WIDTH), lambda i: (0, 0)), tab, tab],
        out_specs=[pl.BlockSpec((tm, SWA_K_COLS), lambda i: (i, 0)),
                   pl.BlockSpec((SWA_KV_WIDTH, tm), lambda i: (0, i))],
        out_shape=[jax.ShapeDtypeStruct((M, SWA_K_COLS), BF16),
                   jax.ShapeDtypeStruct((SWA_KV_WIDTH, M), BF16)],
        scratch_shapes=[pltpu.VMEM((D, 2 * SWA_KV_WIDTH), BF16)],
        compiler_params=_params(1), name="shared_kv",
    )(y, w, cos, sin)


SWA_ONES_ROWS = 16


SWA_GROUP = SWA_WIDTH // SWA_KV_HEADS


def _swa_kernel(sink_ref, y_ref, wq_ref, wg_ref, cos_ref, sin_ref, kp_ref, kc_ref, vtp_ref, vtc_ref, wo_ref,
                o_ref, wo_bf16_ref, wb_ref, *, tiles_per_seq):
    wo_bf16_ref[...] = wo_ref[...].astype(BF16)
    W = WINDOW
    n_pairs = SWA_GROUP // LANES
    QN = n_pairs * W
    g = pl.program_id(0)
    i = pl.program_id(1)
    n_chunks = y_ref.shape[0] // IN_CHUNK
    blocks_per_chunk = IN_CHUNK // W

    @pl.when(i == 0)
    def _cast_weights():
        wb_ref[:, :SWA_GROUP] = wq_ref[...].astype(BF16)
        wb_ref[:, SWA_GROUP:] = wg_ref[...].astype(BF16)

    jrow = lax.broadcasted_iota(jnp.int32, (W, QN), 0)
    icol = lax.broadcasted_iota(jnp.int32, (W, QN), 1)
    upper = jrow > (icol & (W - 1))
    upper_bf16 = jnp.where(upper, 1.0, 0.0).astype(BF16)
    pair_of_lane = lax.broadcasted_iota(jnp.int32, (1, QN), 1) // W
    ones_rows = jnp.ones((SWA_ONES_ROWS, W), BF16)
    no_prev_bias = jnp.where(i % tiles_per_seq == 0, MASK_VALUE, 0.0)

    def project(c):
        rows = pl.ds(c * IN_CHUNK, IN_CHUNK)
        z = jnp.dot(y_ref[rows, :], wb_ref[...], preferred_element_type=F32)
        cos, sin = cos_ref[rows, :], sin_ref[rows, :]
        q = [_swa_rope(z[:, j * LANES:(j + 1) * LANES], cos, sin).astype(BF16) for j in range(n_pairs)]
        sg = _silu(z[:, SWA_GROUP:]).astype(BF16)
        return [(jnp.concatenate([qj[b * W:(b + 1) * W] for qj in q], axis=0), sg[b * W:(b + 1) * W])
                for b in range(blocks_per_chunk)]

    def keys_of(t, parity):
        kcols = slice(parity * LANES, (parity + 1) * LANES)
        prev = kp_ref[:, kcols] if t == 0 else kc_ref[pl.ds((t - 1) * W, W), kcols]
        return [prev, kc_ref[pl.ds(t * W, W), kcols]]

    def scores(t, q):
        keys = jnp.concatenate(keys_of(t, 0) + keys_of(t, 1), axis=0)
        return lax.dot_general(keys, q, NT_DIMS, preferred_element_type=F32)

    def softmax(t, st):
        out = []
        for parity in range(2):
            sp = st[parity * 2 * W:(parity + 1) * 2 * W]
            s_prev = sp[:W] + no_prev_bias if t == 0 else sp[:W]
            s = jnp.where(upper, s_prev, sp[W:])
            sink = jnp.full((1, QN), sink_ref[2 * n_pairs * g + 2 * (n_pairs - 1) + parity], F32)
            for j in range(n_pairs - 1):
                sink = jnp.where(pair_of_lane == j, sink_ref[2 * n_pairs * g + 2 * j + parity], sink)
            sink = sink * LOG2_E
            m = jnp.maximum(jnp.max(s, axis=0, keepdims=True), sink)
            e = jnp.exp2(s - m).astype(BF16)
            e_prev = e * upper_bf16
            out.append((jnp.concatenate([e_prev, e - e_prev], axis=0), jnp.exp2(sink - m)))
        return out

    def finish(t, probs, sg):
        vt_prev = vtp_ref[...] if t == 0 else vtc_ref[:, (t - 1) * W:t * W]
        vt = jnp.concatenate([jnp.concatenate([vt_prev, ones_rows], axis=0),
                              jnp.concatenate([vtc_ref[:, t * W:(t + 1) * W], ones_rows], axis=0)], axis=1)
        ot = []
        for e, sink_term in probs:
            acc = jnp.dot(vt, e, preferred_element_type=F32)
            ot.append(acc[:SWA_DIM] / (acc[SWA_DIM:SWA_DIM + 1] + sink_term))
        for j in range(n_pairs):
            cols = slice(j * LANES, (j + 1) * LANES)
            o = jnp.concatenate([ot[0][:, j * W:(j + 1) * W], ot[1][:, j * W:(j + 1) * W]], axis=0).T
            o_ref[pl.ds(t * W, W), cols] = (o * sg[:, cols].astype(F32)).astype(o_ref.dtype)

    blocks = project(0)
    for c in range(n_chunks):
        ts = [c * blocks_per_chunk + b for b in range(blocks_per_chunk)]
        sts = [scores(t, q) for t, (q, _) in zip(ts, blocks)]
        nxt = project(c + 1) if c + 1 < n_chunks else None
        probs = [softmax(t, st) for t, st in zip(ts, sts)]
        for t, p, (_, sg) in zip(ts, probs, blocks):
            finish(t, p, sg)
        blocks = nxt


def _swa(y, w, layer, cos, sin, k2, vt, sinks, w_out, out_layer, batch, tm=2048):
    M, K = y.shape
    S = M // batch
    tm = min(tm, S)
    n_groups = SWA_KV_HEADS
    blocks_per_tile = tm // WINDOW
    wo_in, wo_out, wo_shape = _side_cast_specs(w_out, out_layer, n_groups * (M // tm), M // tm)

    def before(i):
        return jnp.maximum(i * blocks_per_tile - 1, 0)

    tab = pl.BlockSpec((tm, LANES), lambda g, i: (i, 0))
    return pl.pallas_call(
        functools.partial(_swa_kernel, tiles_per_seq=S // tm),
        grid=(n_groups, M // tm),
        in_specs=[pl.BlockSpec(memory_space=pltpu.SMEM),
                  pl.BlockSpec((tm, K), lambda g, i: (i, 0)),
                  pl.BlockSpec((None, K, SWA_GROUP), lambda g, i: (layer, 0, g)),
                  pl.BlockSpec((None, K, SWA_GROUP), lambda g, i: (layer, 0, n_groups + g)),
                  tab, tab,
                  pl.BlockSpec((WINDOW, 2 * LANES), lambda g, i: (before(i), g)),
                  pl.BlockSpec((tm, 2 * LANES), lambda g, i: (i, g)),
                  pl.BlockSpec((SWA_DIM, WINDOW), lambda g, i: (g, before(i))),
                  pl.BlockSpec((SWA_DIM, tm), lambda g, i: (g, i)), wo_in],
        out_specs=[pl.BlockSpec((tm, SWA_GROUP), lambda g, i: (i, g)), wo_out],
        out_shape=[jax.ShapeDtypeStruct((M, SWA_WIDTH), BF16), wo_shape],
        scratch_shapes=[pltpu.VMEM((K, 2 * SWA_GROUP), BF16)],
        compiler_params=_params(2), name="swa",
    )(sinks, y, w, w, cos, sin, k2, k2, vt, vt, w_out)


OUT_CHUNK = 128


def _out_kernel(a1_ref, a2_ref, w1_ref, w2_ref, h_ref, pg_ref, *rest, n_next):
    ng_ref = rest[0] if n_next else None
    h_out_ref = rest[1] if n_next else rest[0]
    y_refs = rest[2:2 + n_next]
    for r in range(0, a1_ref.shape[0], OUT_CHUNK):
        rows = pl.ds(r, OUT_CHUNK)
        acc = (jnp.dot(a1_ref[rows, :], w1_ref[...], preferred_element_type=F32)
               + jnp.dot(a2_ref[rows, :], w2_ref[...], preferred_element_type=F32))
        h = h_ref[rows, :] + _rms(acc) * pg_ref[...]
        h_out_ref[rows, :] = h
        if n_next:
            hn = _rms(h)
            for g in range(n_next):
                y_refs[g][rows, :] = (hn * ng_ref[g]).astype(y_refs[g].dtype)


def _mix_out(o_main, o_mem, w_out, h, post_g, next_g, tm=512):
    M = h.shape[0]
    tm = min(tm, M)
    n_next = len(next_g)
    k1, k2 = o_main.shape[1], o_mem.shape[1]
    assert k1 % k2 == 0

    def row(width):
        return pl.BlockSpec((tm, width), lambda i: (i, 0))

    once = pl.Buffered(1)
    in_specs = [row(k1), row(k2),
                pl.BlockSpec((k1, D_MODEL), lambda i: (0, 0), pipeline_mode=once),
                pl.BlockSpec((k2, D_MODEL), lambda i: (k1 // k2, 0), pipeline_mode=once),
                row(D_MODEL),
                pl.BlockSpec((1, D_MODEL), lambda i: (0, 0))]
    args = [o_main, o_mem, w_out, w_out, h, post_g[None]]
    if n_next:
        in_specs.append(pl.BlockSpec((n_next, 1, D_MODEL), lambda i: (0, 0, 0)))
        args.append(jnp.stack(next_g)[:, None, :])
    out_specs = [row(D_MODEL)] * (1 + n_next)
    out_shape = [jax.ShapeDtypeStruct((M, D_MODEL), F32)] + [jax.ShapeDtypeStruct((M, D_MODEL), BF16)] * n_next
    res = pl.pallas_call(
        functools.partial(_out_kernel, n_next=n_next), grid=(M // tm,),
        in_specs=in_specs, out_specs=out_specs, out_shape=out_shape,
        compiler_params=_params(1), name="mix_out",
    )(*args)
    return res[0], res[1:]


def kernel(x, mem, positions, pre_norm_g, post_norm_g, mem_norm_g, kv_norm_g,
           w_in_a, w_in_b, w_kv_b, sinks_b, w_mem_kv, w_out):
    B, S, D = x.shape
    Mm = mem.shape[1]
    M = B * S
    h = x.reshape(M, D)
    y, cos_r, sin_r, cos_k, sin_k, cos_q, sin_q = _prologue(h, pre_norm_g[0], positions.astype(F32).reshape(M, 1))
    mkv = _mem_kv(mem.reshape(B * Mm, D), mem_norm_g, w_mem_kv)
    k2 = vt = None
    for l in range(DEPTH):
        if l < N_A:
            o_mem = _mem_mixer(y, w_in_a, l, MEM_GROUP_A, mkv, l, B)
            o_main, w_out_l = _retention(y, w_in_a, l, cos_r, sin_r, w_out, B)
        else:
            o_mem = _mem_mixer(y, w_in_b, l - N_A, MEM_GROUP_B, mkv, l, B)
            o_main, w_out_l = _swa(y, w_in_b, l - N_A, cos_q, sin_q, k2, vt, sinks_b[l - N_A], w_out, l, B)
        next_g = [pre_norm_g[l + 1]] if l < DEPTH - 1 else []
        if l == N_A - 1:
            next_g.append(kv_norm_g)
        h, ys = _mix_out(o_main, o_mem, w_out_l, h, post_norm_g[l], next_g)
        if ys:
            y = ys[0]
        if l == N_A - 1:
            k2, vt = _shared_kv(ys[1], w_kv_b, cos_k, sin_k)
    return h.reshape(B, S, D)
```

```python
import functools

import jax
import jax.numpy as jnp
from jax import lax
from jax.experimental import pallas as pl
from jax.experimental.pallas import tpu as pltpu

F32 = jnp.float32
BF16 = jnp.bfloat16

D_MODEL = 2048
DEPTH = 4
N_A = DEPTH // 2
RET_DIM = 256
RET_HEADS = D_MODEL // RET_DIM
RET_WIDTH = RET_HEADS * RET_DIM
RET_CHUNK = 256
RET_THETA = 10000.0
SWA_DIM = 64
SWA_HEADS = D_MODEL // SWA_DIM
SWA_KV_HEADS = SWA_HEADS // 8
SWA_WIDTH = SWA_HEADS * SWA_DIM
SWA_KV_WIDTH = SWA_KV_HEADS * SWA_DIM
WINDOW = 128
ROPE_THETA = 500000.0
ROT_DIM = SWA_DIM // 4
MEM_HEADS = 4
MEM_DIM = D_MODEL // 8
MEM_WIDTH = MEM_HEADS * MEM_DIM
EPS = 1e-6
MASK_VALUE = -1e30
LOG2_E = 1.4426950408889634

LANES = 128
VMEM_LIMIT = 56 * 1024 * 1024

NT_DIMS = (((1,), (1,)), ((), ()))
TN_DIMS = (((0,), (0,)), ((), ()))

MEM_GROUP_A = 4 * RET_WIDTH // MEM_WIDTH
MEM_GROUP_B = 2 * SWA_WIDTH // MEM_WIDTH


def _params(n_axes):
    return pltpu.CompilerParams(dimension_semantics=("arbitrary",) * n_axes,
                                vmem_limit_bytes=VMEM_LIMIT)


def _silu(g):
    return g * (1.0 / (1.0 + jnp.exp(-g)))


def _rms(x):
    return x * lax.rsqrt(jnp.mean(x * x, axis=-1, keepdims=True) + EPS)


def _prologue_kernel(x_ref, g_ref, pos_ref, inv_r_ref, inv_s_ref, sgn_ref,
                     y_ref, cr_ref, sr_ref, ck_ref, sk_ref, cq_ref, sq_ref):
    y_ref[...] = (_rms(x_ref[...]) * g_ref[...]).astype(y_ref.dtype)
    pos = pos_ref[...]
    ang_r = pos * inv_r_ref[...]
    cr_ref[...] = jnp.cos(ang_r)
    sr_ref[...] = jnp.sin(ang_r)
    ang_s = pos * inv_s_ref[...]
    groups = SWA_DIM // ROT_DIM
    rows = ang_s.shape[0] // groups
    packed = ang_s[:rows]
    for b in range(1, groups):
        packed = packed + pltpu.roll(ang_s[b * rows:(b + 1) * rows], b * ROT_DIM, 1)
    cos_p, sin_p = jnp.cos(packed), jnp.sin(packed)
    rotary_lane = (lax.broadcasted_iota(jnp.int32, packed.shape, 1) & (SWA_DIM - 1)) < ROT_DIM
    for b in range(groups):
        unroll = (lambda t: pltpu.roll(t, LANES - b * ROT_DIM, 1)) if b else (lambda t: t)
        cos_s = jnp.where(rotary_lane, unroll(cos_p), 1.0)
        sin_s = unroll(sin_p) * sgn_ref[...]
        part = pl.ds(b * rows, rows)
        ck_ref[part, :] = cos_s
        sk_ref[part, :] = sin_s
        cq_ref[part, :] = cos_s * (LOG2_E * SWA_DIM ** -0.5)
        sq_ref[part, :] = sin_s * (LOG2_E * SWA_DIM ** -0.5)


def _prologue(x, gain, pos, tm=512):
    M, D = x.shape
    tm = min(tm, M)
    inv_r = RET_THETA ** (-jnp.arange(0, RET_DIM, 2, dtype=F32) / RET_DIM)
    inv_h = ROPE_THETA ** (-jnp.arange(0, ROT_DIM, 2, dtype=F32) / ROT_DIM)
    half = ROT_DIM // 2
    head = jnp.concatenate([inv_h, inv_h, jnp.zeros((SWA_DIM - ROT_DIM,), F32)])
    sgn_h = jnp.concatenate([-jnp.ones((half,), F32), jnp.ones((half,), F32),
                             jnp.zeros((SWA_DIM - ROT_DIM,), F32)])
    inv_s = jnp.tile(head, LANES // SWA_DIM)
    sgn = jnp.tile(sgn_h, LANES // SWA_DIM)
    row = pl.BlockSpec((1, LANES), lambda i: (0, 0))
    tab = pl.BlockSpec((tm, LANES), lambda i: (i, 0))
    shape = jax.ShapeDtypeStruct((M, LANES), F32)
    wide = pl.BlockSpec((tm, D), lambda i: (i, 0))
    return pl.pallas_call(
        _prologue_kernel, grid=(M // tm,),
        in_specs=[wide, pl.BlockSpec((1, D), lambda i: (0, 0)), pl.BlockSpec((tm, 1), lambda i: (i, 0)),
                  row, row, row],
        out_specs=[wide] + [tab] * 6,
        out_shape=[jax.ShapeDtypeStruct((M, D), BF16)] + [shape] * 6,
        compiler_params=_params(1), name="prologue",
    )(x, gain[None], pos, inv_r[None], inv_s[None], sgn[None])


def _mem_kv_kernel(mem_ref, g_ref, w_ref, o_ref, a_ref):
    @pl.when(pl.program_id(1) == 0)
    def _norm():
        a_ref[...] = (_rms(mem_ref[...]) * g_ref[...]).astype(a_ref.dtype)

    acc = jnp.dot(a_ref[...], w_ref[...].astype(BF16), preferred_element_type=F32)
    scale = jnp.where(pl.program_id(1) * 2 < pl.num_programs(1), MEM_DIM ** -0.5, 1.0)
    o_ref[...] = (acc * scale).astype(o_ref.dtype)


def _mem_kv(mem, gains, w, tn=1024):
    R, D = mem.shape
    L, _, N = w.shape
    return pl.pallas_call(
        _mem_kv_kernel, grid=(L, N // tn),
        in_specs=[pl.BlockSpec((R, D), lambda l, j: (0, 0)),
                  pl.BlockSpec((None, 1, D), lambda l, j: (l, 0, 0)),
                  pl.BlockSpec((None, D, tn), lambda l, j: (l, 0, j))],
        out_specs=pl.BlockSpec((None, R, tn), lambda l, j: (l, 0, j)),
        out_shape=jax.ShapeDtypeStruct((L, R, N), BF16),
        scratch_shapes=[pltpu.VMEM((R, D), BF16)],
        compiler_params=_params(2), name="mem_kv",
    )(mem, gains[:, None, :], w)


def _swa_rope(x, cos, sin_signed):
    lane = lax.broadcasted_iota(jnp.int32, x.shape, 1)
    first_half = (lane & (SWA_DIM - 1)) < (ROT_DIM // 2)
    partner = jnp.where(first_half, pltpu.roll(x, LANES - ROT_DIM // 2, 1), pltpu.roll(x, ROT_DIM // 2, 1))
    return x * cos + partner * sin_signed


IN_CHUNK = 256


def _mem_mixer_kernel(y_ref, wq_ref, wg_ref, mk_ref, mv_ref, o_ref, wb_ref):
    D = MEM_DIM

    @pl.when(pl.program_id(1) == 0)
    def _cast_weights():
        wb_ref[:, :D] = wq_ref[...].astype(BF16)
        wb_ref[:, D:] = wg_ref[...].astype(BF16)

    def project(c):
        z = jnp.dot(y_ref[pl.ds(c * IN_CHUNK, IN_CHUNK), :], wb_ref[...], preferred_element_type=F32)
        return z[:, :D].astype(BF16), _silu(z[:, D:])

    n_chunks = y_ref.shape[0] // IN_CHUNK
    nxt = project(0)
    for c in range(n_chunks):
        q, sg = nxt
        s = lax.dot_general(q, mk_ref[...], NT_DIMS, preferred_element_type=F32)
        if c + 1 < n_chunks:
            nxt = project(c + 1)
        e = jnp.exp(s - jnp.max(s, axis=-1, keepdims=True))
        p = (e / jnp.sum(e, axis=-1, keepdims=True)).astype(BF16)
        o = jnp.dot(p, mv_ref[...], preferred_element_type=F32)
        o_ref[pl.ds(c * IN_CHUNK, IN_CHUNK), :] = (o * sg).astype(o_ref.dtype)


def _mem_mixer(y, w, layer, q_group, mkv, mkv_layer, batch, tm=4096):
    M, K = y.shape
    S = M // batch
    tm = min(tm, S)
    tiles_per_seq = S // tm
    Mm = mkv.shape[1] // batch
    H = MEM_HEADS

    def wcol(group):
        return pl.BlockSpec((None, K, MEM_DIM), lambda m, i: (layer, 0, group * H + m))

    def mem(off):
        return pl.BlockSpec((None, Mm, MEM_DIM), lambda m, i: (mkv_layer, i // tiles_per_seq, off * H + m))

    return pl.pallas_call(
        _mem_mixer_kernel, grid=(H, M // tm),
        in_specs=[pl.BlockSpec((tm, K), lambda m, i: (i, 0)), wcol(q_group), wcol(q_group + 1), mem(0), mem(1)],
        out_specs=pl.BlockSpec((tm, MEM_DIM), lambda m, i: (i, m)),
        out_shape=jax.ShapeDtypeStruct((M, MEM_WIDTH), BF16),
        scratch_shapes=[pltpu.VMEM((K, 2 * MEM_DIM), BF16)],
        compiler_params=_params(2), name="mem_mixer",
    )(y, w, w, mkv, mkv)


def _side_cast_specs(w_out, layer, n_steps, steps_per_row):
    rows, cols = w_out.shape[1:]
    slab = rows // n_steps
    assert slab * n_steps == rows and slab % 16 == 0, (rows, n_steps)
    return (pl.BlockSpec((None, slab, cols), lambda a, b: (layer, a * steps_per_row + b, 0)),
            pl.BlockSpec((slab, cols), lambda a, b: (a * steps_per_row + b, 0)),
            jax.ShapeDtypeStruct((rows, cols), BF16))


def _retention_kernel(y_ref, wq_ref, wk_ref, wv_ref, wg_ref, cos_ref, sin_ref, wo_ref, o_ref, wo_bf16_ref,
                      wb_ref, r_ref, dec_ref, xi_ref, zeta_ref, gc_ref, *, tiles_per_seq):
    C = RET_CHUNK
    D = RET_DIM
    h = pl.program_id(0)
    i = pl.program_id(1)
    n_chunks = y_ref.shape[0] // C
    k_scale = RET_DIM ** -0.5
    wo_bf16_ref[...] = wo_ref[...].astype(BF16)

    @pl.when(i == 0)
    def _init():
        def log_gamma(shape):
            return jnp.log(1.0 - jnp.exp2(-5.0 - jnp.full(shape, h, jnp.int32).astype(F32)))

        for c, w_ref in enumerate((wq_ref, wk_ref, wv_ref, wg_ref)):
            wb_ref[:, c * D:(c + 1) * D] = w_ref[...].astype(BF16)
        n = lax.broadcasted_iota(jnp.int32, (C, RET_DIM), 0).astype(F32)
        xi_ref[...] = jnp.exp(log_gamma((C, RET_DIM)) * (n + 1.0))
        zeta_ref[...] = jnp.exp(log_gamma((C, RET_DIM)) * (C - 1.0 - n)) * k_scale
        gc_ref[...] = jnp.exp(log_gamma(gc_ref.shape) * float(C))
        diff = (lax.broadcasted_iota(jnp.int32, (C, C), 0)
                - lax.broadcasted_iota(jnp.int32, (C, C), 1)).astype(F32)
        dec_ref[...] = jnp.where(diff >= 0, jnp.exp(log_gamma((C, C)) * jnp.maximum(diff, 0.0)) * k_scale, 0.0)

    @pl.when(i % tiles_per_seq == 0)
    def _reset_state():
        r_ref[...] = jnp.zeros_like(r_ref)

    half = D // 2

    def rope(x, cos, sin):
        x1, x2 = x[:, :half], x[:, half:]
        return jnp.concatenate([x1 * cos - x2 * sin, x2 * cos + x1 * sin], axis=-1)

    def project(c):
        rows = pl.ds(c * C, C)
        z = jnp.dot(y_ref[rows, :], wb_ref[...], preferred_element_type=F32)
        cos, sin = cos_ref[rows, :], sin_ref[rows, :]
        k = rope(z[:, D:2 * D], cos, sin)
        return (rope(z[:, :D], cos, sin).astype(BF16), k.astype(BF16), (k * zeta_ref[...]).astype(BF16),
                z[:, 2 * D:3 * D].astype(BF16), _silu(z[:, 3 * D:]).astype(BF16))

    r = r_ref[...]
    nxt = project(0)
    for c in range(n_chunks):
        q, k, kz, v, sg = nxt
        if c + 1 < n_chunks:
            nxt = project(c + 1)
        inner = lax.dot_general(q, k, NT_DIMS, preferred_element_type=F32)
        update = lax.dot_general(kz, v, TN_DIMS, preferred_element_type=F32)
        cross = jnp.dot(q, r.astype(BF16), preferred_element_type=F32)
        o = jnp.dot((inner * dec_ref[...]).astype(BF16), v, preferred_element_type=F32) + xi_ref[...] * cross
        r = r * gc_ref[0:1, :] + update
        o_ref[pl.ds(c * C, C), :] = (_rms(o) * sg.astype(F32)).astype(o_ref.dtype)
    r_ref[...] = r


def _retention(y, w, layer, cos, sin, w_out, batch, tm=2048):
    M, K = y.shape
    S = M // batch
    tm = min(tm, S)
    H = RET_HEADS
    wo_in, wo_out, wo_shape = _side_cast_specs(w_out, layer, H * (M // tm), M // tm)

    def wcol(group):
        return pl.BlockSpec((None, K, RET_DIM), lambda h, i: (layer, 0, group * H + h))

    tab = pl.BlockSpec((tm, LANES), lambda h, i: (i, 0))
    return pl.pallas_call(
        functools.partial(_retention_kernel, tiles_per_seq=S // tm),
        grid=(H, M // tm),
        in_specs=[pl.BlockSpec((tm, K), lambda h, i: (i, 0)), wcol(0), wcol(1), wcol(2), wcol(3), tab, tab, wo_in],
        out_specs=[pl.BlockSpec((tm, RET_DIM), lambda h, i: (i, h)), wo_out],
        out_shape=[jax.ShapeDtypeStruct((M, RET_WIDTH), BF16), wo_shape],
        scratch_shapes=[pltpu.VMEM((K, 4 * RET_DIM), BF16),
                        pltpu.VMEM((RET_DIM, RET_DIM), F32),
                        pltpu.VMEM((RET_CHUNK, RET_CHUNK), F32),
                        pltpu.VMEM((RET_CHUNK, RET_DIM), F32),
                        pltpu.VMEM((RET_CHUNK, RET_DIM), F32),
                        pltpu.VMEM((8, RET_DIM), F32)],
        compiler_params=_params(2), name="retention",
    )(y, w, w, w, w, cos, sin, w_out)


def _kv_kernel(y_ref, w_ref, cos_ref, sin_ref, k_ref, vt_ref, wb_ref):
    @pl.when(pl.program_id(0) == 0)
    def _cast_weight():
        wb_ref[...] = w_ref[...].astype(BF16)

    low = lax.broadcasted_iota(jnp.int32, (IN_CHUNK, LANES), 1) < SWA_DIM
    for r in range(0, y_ref.shape[0], IN_CHUNK):
        rows = pl.ds(r, IN_CHUNK)
        kv = jnp.dot(y_ref[rows, :], wb_ref[...], preferred_element_type=F32)
        cos, sin = cos_ref[rows, :], sin_ref[rows, :]
        for blk in range(SWA_KV_WIDTH // LANES):
            k = _swa_rope(kv[:, blk * LANES:(blk + 1) * LANES], cos, sin)
            swapped = pltpu.roll(k, SWA_DIM, 1)
            tiles = (jnp.where(low, k, 0.0), jnp.where(low, 0.0, swapped),
                     jnp.where(low, swapped, 0.0), jnp.where(low, 0.0, k))
            for i, t in enumerate(tiles):
                k_ref[rows, (4 * blk + i) * LANES:(4 * blk + i + 1) * LANES] = t.astype(k_ref.dtype)
        vt_ref[:, r:r + IN_CHUNK] = kv[:, SWA_KV_WIDTH:].T.astype(vt_ref.dtype)


SWA_K_COLS = 2 * LANES * SWA_KV_HEADS


def _shared_kv(y, w, cos, sin, tm=2048):
    M, D = y.shape
    tm = min(tm, M)
    tab = pl.BlockSpec((tm, LANES), lambda i: (i, 0))
    return pl.pallas_call(
        _kv_kernel, grid=(M // tm,),
        in_specs=[pl.BlockSpec((tm, D), lambda i: (i, 0)),
                  pl.BlockSpec((D, 2 * SWA_KV_WIDTH), lambda i: (0, 0)), tab, tab],
        out_specs=[pl.BlockSpec((tm, SWA_K_COLS), lambda i: (i, 0)),
                   pl.BlockSpec((SWA_KV_WIDTH, tm), lambda i: (0, i))],
        out_shape=[jax.ShapeDtypeStruct((M, SWA_K_COLS), BF16),
                   jax.ShapeDtypeStruct((SWA_KV_WIDTH, M), BF16)],
        scratch_shapes=[pltpu.VMEM((D, 2 * SWA_KV_WIDTH), BF16)],
        compiler_params=_params(1), name="shared_kv",
    )(y, w, cos, sin)


SWA_ONES_ROWS = 16


SWA_GROUP = SWA_WIDTH // SWA_KV_HEADS


def _swa_kernel(sink_ref, y_ref, wq_ref, wg_ref, cos_ref, sin_ref, kp_ref, kc_ref, vtp_ref, vtc_ref, wo_ref,
                o_ref, wo_bf16_ref, wb_ref, *, tiles_per_seq):
    wo_bf16_ref[...] = wo_ref[...].astype(BF16)
    W = WINDOW
    n_pairs = SWA_GROUP // LANES
    QN = n_pairs * W
    g = pl.program_id(0)
    i = pl.program_id(1)
    n_chunks = y_ref.shape[0] // IN_CHUNK
    blocks_per_chunk = IN_CHUNK // W

    @pl.when(i == 0)
    def _cast_weights():
        wb_ref[:, :SWA_GROUP] = wq_ref[...].astype(BF16)
        wb_ref[:, SWA_GROUP:] = wg_ref[...].astype(BF16)

    jrow = lax.broadcasted_iota(jnp.int32, (W, QN), 0)
    icol = lax.broadcasted_iota(jnp.int32, (W, QN), 1)
    upper = jrow > (icol & (W - 1))
    upper_bf16 = jnp.where(upper, 1.0, 0.0).astype(BF16)
    pair_of_lane = lax.broadcasted_iota(jnp.int32, (1, QN), 1) // W
    ones_rows = jnp.ones((SWA_ONES_ROWS, W), BF16)
    no_prev_bias = jnp.where(i % tiles_per_seq == 0, MASK_VALUE, 0.0)

    def project(c):
        rows = pl.ds(c * IN_CHUNK, IN_CHUNK)
        z = jnp.dot(y_ref[rows, :], wb_ref[...], preferred_element_type=F32)
        cos, sin = cos_ref[rows, :], sin_ref[rows, :]
        q = [_swa_rope(z[:, j * LANES:(j + 1) * LANES], cos, sin).astype(BF16) for j in range(n_pairs)]
        sg = _silu(z[:, SWA_GROUP:]).astype(BF16)
        return [(jnp.concatenate([qj[b * W:(b + 1) * W] for qj in q], axis=0), sg[b * W:(b + 1) * W])
                for b in range(blocks_per_chunk)]

    def keys_of(t, parity):
        kcols = slice(parity * LANES, (parity + 1) * LANES)
        prev = kp_ref[:, kcols] if t == 0 else kc_ref[pl.ds((t - 1) * W, W), kcols]
        return [prev, kc_ref[pl.ds(t * W, W), kcols]]

    def scores(t, q):
        keys = jnp.concatenate(keys_of(t, 0) + keys_of(t, 1), axis=0)
        return lax.dot_general(keys, q, NT_DIMS, preferred_element_type=F32)

    def softmax(t, st):
        out = []
        for parity in range(2):
            sp = st[parity * 2 * W:(parity + 1) * 2 * W]
            s_prev = sp[:W] + no_prev_bias if t == 0 else sp[:W]
            s = jnp.where(upper, s_prev, sp[W:])
            sink = jnp.full((1, QN), sink_ref[2 * n_pairs * g + 2 * (n_pairs - 1) + parity], F32)
            for j in range(n_pairs - 1):
                sink = jnp.where(pair_of_lane == j, sink_ref[2 * n_pairs * g + 2 * j + parity], sink)
            sink = sink * LOG2_E
            m = jnp.maximum(jnp.max(s, axis=0, keepdims=True), sink)
            e = jnp.exp2(s - m).astype(BF16)
            e_prev = e * upper_bf16
            out.append((jnp.concatenate([e_prev, e - e_prev], axis=0), jnp.exp2(sink - m)))
        return out

    def finish(t, probs, sg):
        vt_prev = vtp_ref[...] if t == 0 else vtc_ref[:, (t - 1) * W:t * W]
        vt = jnp.concatenate([jnp.concatenate([vt_prev, ones_rows], axis=0),
                              jnp.concatenate([vtc_ref[:, t * W:(t + 1) * W], ones_rows], axis=0)], axis=1)
        ot = []
        for e, sink_term in probs:
            acc = jnp.dot(vt, e, preferred_element_type=F32)
            ot.append(acc[:SWA_DIM] / (acc[SWA_DIM:SWA_DIM + 1] + sink_term))
        for j in range(n_pairs):
            cols = slice(j * LANES, (j + 1) * LANES)
            o = jnp.concatenate([ot[0][:, j * W:(j + 1) * W], ot[1][:, j * W:(j + 1) * W]], axis=0).T
            o_ref[pl.ds(t * W, W), cols] = (o * sg[:, cols].astype(F32)).astype(o_ref.dtype)

    blocks = project(0)
    for c in range(n_chunks):
        ts = [c * blocks_per_chunk + b for b in range(blocks_per_chunk)]
        sts = [scores(t, q) for t, (q, _) in zip(ts, blocks)]
        nxt = project(c + 1) if c + 1 < n_chunks else None
        probs = [softmax(t, st) for t, st in zip(ts, sts)]
        for t, p, (_, sg) in zip(ts, probs, blocks):
            finish(t, p, sg)
        blocks = nxt


def _swa(y, w, layer, cos, sin, k2, vt, sinks, w_out, out_layer, batch, tm=2048):
    M, K = y.shape
    S = M // batch
    tm = min(tm, S)
    n_groups = SWA_KV_HEADS
    blocks_per_tile = tm // WINDOW
    wo_in, wo_out, wo_shape = _side_cast_specs(w_out, out_layer, n_groups * (M // tm), M // tm)

    def before(i):
        return jnp.maximum(i * blocks_per_tile - 1, 0)

    tab = pl.BlockSpec((tm, LANES), lambda g, i: (i, 0))
    return pl.pallas_call(
        functools.partial(_swa_kernel, tiles_per_seq=S // tm),
        grid=(n_groups, M // tm),
        in_specs=[pl.BlockSpec(memory_space=pltpu.SMEM),
                  pl.BlockSpec((tm, K), lambda g, i: (i, 0)),
                  pl.BlockSpec((None, K, SWA_GROUP), lambda g, i: (layer, 0, g)),
                  pl.BlockSpec((None, K, SWA_GROUP), lambda g, i: (layer, 0, n_groups + g)),
                  tab, tab,
                  pl.BlockSpec((WINDOW, 2 * LANES), lambda g, i: (before(i), g)),
                  pl.BlockSpec((tm, 2 * LANES), lambda g, i: (i, g)),
                  pl.BlockSpec((SWA_DIM, WINDOW), lambda g, i: (g, before(i))),
                  pl.BlockSpec((SWA_DIM, tm), lambda g, i: (g, i)), wo_in],
        out_specs=[pl.BlockSpec((tm, SWA_GROUP), lambda g, i: (i, g)), wo_out],
        out_shape=[jax.ShapeDtypeStruct((M, SWA_WIDTH), BF16), wo_shape],
        scratch_shapes=[pltpu.VMEM((K, 2 * SWA_GROUP), BF16)],
        compiler_params=_params(2), name="swa",
    )(sinks, y, w, w, cos, sin, k2, k2, vt, vt, w_out)


OUT_CHUNK = 128


def _out_kernel(a1_ref, a2_ref, w1_ref, w2_ref, h_ref, pg_ref, *rest, n_next):
    ng_ref = rest[0] if n_next else None
    h_out_ref = rest[1] if n_next else rest[0]
    y_refs = rest[2:2 + n_next]
    for r in range(0, a1_ref.shape[0], OUT_CHUNK):
        rows = pl.ds(r, OUT_CHUNK)
        acc = (jnp.dot(a1_ref[rows, :], w1_ref[...], preferred_element_type=F32)
               + jnp.dot(a2_ref[rows, :], w2_ref[...], preferred_element_type=F32))
        h = h_ref[rows, :] + _rms(acc) * pg_ref[...]
        h_out_ref[rows, :] = h
        if n_next:
            hn = _rms(h)
            for g in range(n_next):
                y_refs[g][rows, :] = (hn * ng_ref[g]).astype(y_refs[g].dtype)


def _mix_out(o_main, o_mem, w_out, h, post_g, next_g, tm=512):
    M = h.shape[0]
    tm = min(tm, M)
    n_next = len(next_g)
    k1, k2 = o_main.shape[1], o_mem.shape[1]
    assert k1 % k2 == 0

    def row(width):
        return pl.BlockSpec((tm, width), lambda i: (i, 0))

    once = pl.Buffered(1)
    in_specs = [row(k1), row(k2),
                pl.BlockSpec((k1, D_MODEL), lambda i: (0, 0), pipeline_mode=once),
                pl.BlockSpec((k2, D_MODEL), lambda i: (k1 // k2, 0), pipeline_mode=once),
                row(D_MODEL),
                pl.BlockSpec((1, D_MODEL), lambda i: (0, 0))]
    args = [o_main, o_mem, w_out, w_out, h, post_g[None]]
    if n_next:
        in_specs.append(pl.BlockSpec((n_next, 1, D_MODEL), lambda i: (0, 0, 0)))
        args.append(jnp.stack(next_g)[:, None, :])
    out_specs = [row(D_MODEL)] * (1 + n_next)
    out_shape = [jax.ShapeDtypeStruct((M, D_MODEL), F32)] + [jax.ShapeDtypeStruct((M, D_MODEL), BF16)] * n_next
    res = pl.pallas_call(
        functools.partial(_out_kernel, n_next=n_next), grid=(M // tm,),
        in_specs=in_specs, out_specs=out_specs, out_shape=out_shape,
        compiler_params=_params(1), name="mix_out",
    )(*args)
    return res[0], res[1:]


def kernel(x, mem, positions, pre_norm_g, post_norm_g, mem_norm_g, kv_norm_g,
           w_in_a, w_in_b, w_kv_b, sinks_b, w_mem_kv, w_out):
    B, S, D = x.shape
    Mm = mem.shape[1]
    M = B * S
    h = x.reshape(M, D)
    y, cos_r, sin_r, cos_k, sin_k, cos_q, sin_q = _prologue(h, pre_norm_g[0], positions.astype(F32).reshape(M, 1))
    mkv = _mem_kv(mem.reshape(B * Mm, D), mem_norm_g, w_mem_kv)
    k2 = vt = None
    for l in range(DEPTH):
        if l < N_A:
            o_mem = _mem_mixer(y, w_in_a, l, MEM_GROUP_A, mkv, l, B)
            o_main, w_out_l = _retention(y, w_in_a, l, cos_r, sin_r, w_out, B)
        else:
            o_mem = _mem_mixer(y, w_in_b, l - N_A, MEM_GROUP_B, mkv, l, B)
            o_main, w_out_l = _swa(y, w_in_b, l - N_A, cos_q, sin_q, k2, vt, sinks_b[l - N_A], w_out, l, B)
        next_g = [pre_norm_g[l + 1]] if l < DEPTH - 1 else []
        if l == N_A - 1:
            next_g.append(kv_norm_g)
        h, ys = _mix_out(o_main, o_mem, w_out_l, h, post_norm_g[l], next_g)
        if ys:
            y = ys[0]
        if l == N_A - 1:
            k2, vt = _shared_kv(ys[1], w_kv_b, cos_k, sin_k)
    return h.reshape(B, S, D)
```

```python
import functools

import jax
import jax.numpy as jnp
from jax import lax
from jax.experimental import pallas as pl
from jax.experimental.pallas import tpu as pltpu

F32 = jnp.float32
BF16 = jnp.bfloat16

D_MODEL = 2048
DEPTH = 4
N_A = DEPTH // 2
RET_DIM = 256
RET_HEADS = D_MODEL // RET_DIM
RET_WIDTH = RET_HEADS * RET_DIM
RET_CHUNK = 256
RET_THETA = 10000.0
SWA_DIM = 64
SWA_HEADS = D_MODEL // SWA_DIM
SWA_KV_HEADS = SWA_HEADS // 8
SWA_WIDTH = SWA_HEADS * SWA_DIM
SWA_KV_WIDTH = SWA_KV_HEADS * SWA_DIM
WINDOW = 128
ROPE_THETA = 500000.0
ROT_DIM = SWA_DIM // 4
MEM_HEADS = 4
MEM_DIM = D_MODEL // 8
MEM_WIDTH = MEM_HEADS * MEM_DIM
EPS = 1e-6
MASK_VALUE = -1e30
LOG2_E = 1.4426950408889634

LANES = 128
VMEM_LIMIT = 60 * 1024 * 1024

NT_DIMS = (((1,), (1,)), ((), ()))
TN_DIMS = (((0,), (0,)), ((), ()))

MEM_GROUP_A = 4 * RET_WIDTH // MEM_WIDTH
MEM_GROUP_B = 2 * SWA_WIDTH // MEM_WIDTH


def _params(n_axes):
    return pltpu.CompilerParams(dimension_semantics=("arbitrary",) * n_axes,
                                vmem_limit_bytes=VMEM_LIMIT)


def _silu(g):
    return g * (1.0 / (1.0 + jnp.exp(-g)))


def _rms(x):
    return x * lax.rsqrt(jnp.mean(x * x, axis=-1, keepdims=True) + EPS)


def _prologue_kernel(x_ref, g_ref, pos_ref, inv_r_ref, inv_s_ref, sgn_ref,
                     y_ref, cr_ref, sr_ref, ck_ref, sk_ref, cq_ref, sq_ref):
    y_ref[...] = (_rms(x_ref[...]) * g_ref[...]).astype(y_ref.dtype)
    pos = pos_ref[...]
    ang_r = pos * inv_r_ref[...]
    cr_ref[...] = jnp.cos(ang_r)
    sr_ref[...] = jnp.sin(ang_r)
    ang_s = pos * inv_s_ref[...]
    groups = SWA_DIM // ROT_DIM
    rows = ang_s.shape[0] // groups
    packed = ang_s[:rows]
    for b in range(1, groups):
        packed = packed + pltpu.roll(ang_s[b * rows:(b + 1) * rows], b * ROT_DIM, 1)
    cos_p, sin_p = jnp.cos(packed), jnp.sin(packed)
    rotary_lane = (lax.broadcasted_iota(jnp.int32, packed.shape, 1) & (SWA_DIM - 1)) < ROT_DIM
    for b in range(groups):
        unroll = (lambda t: pltpu.roll(t, LANES - b * ROT_DIM, 1)) if b else (lambda t: t)
        cos_s = jnp.where(rotary_lane, unroll(cos_p), 1.0)
        sin_s = unroll(sin_p) * sgn_ref[...]
        part = pl.ds(b * rows, rows)
        ck_ref[part, :] = cos_s
        sk_ref[part, :] = sin_s
        cq_ref[part, :] = cos_s * (LOG2_E * SWA_DIM ** -0.5)
        sq_ref[part, :] = sin_s * (LOG2_E * SWA_DIM ** -0.5)


def _prologue(x, gain, pos, tm=1024):
    M, D = x.shape
    tm = min(tm, M)
    inv_r = RET_THETA ** (-jnp.arange(0, RET_DIM, 2, dtype=F32) / RET_DIM)
    inv_h = ROPE_THETA ** (-jnp.arange(0, ROT_DIM, 2, dtype=F32) / ROT_DIM)
    half = ROT_DIM // 2
    head = jnp.concatenate([inv_h, inv_h, jnp.zeros((SWA_DIM - ROT_DIM,), F32)])
    sgn_h = jnp.concatenate([-jnp.ones((half,), F32), jnp.ones((half,), F32),
                             jnp.zeros((SWA_DIM - ROT_DIM,), F32)])
    inv_s = jnp.tile(head, LANES // SWA_DIM)
    sgn = jnp.tile(sgn_h, LANES // SWA_DIM)
    row = pl.BlockSpec((1, LANES), lambda i: (0, 0))
    tab = pl.BlockSpec((tm, LANES), lambda i: (i, 0))
    shape = jax.ShapeDtypeStruct((M, LANES), F32)
    wide = pl.BlockSpec((tm, D), lambda i: (i, 0))
    return pl.pallas_call(
        _prologue_kernel, grid=(M // tm,),
        in_specs=[wide, pl.BlockSpec((1, D), lambda i: (0, 0)), pl.BlockSpec((tm, 1), lambda i: (i, 0)),
                  row, row, row],
        out_specs=[wide] + [tab] * 6,
        out_shape=[jax.ShapeDtypeStruct((M, D), BF16)] + [shape] * 6,
        compiler_params=_params(1), name="prologue",
    )(x, gain[None], pos, inv_r[None], inv_s[None], sgn[None])


def _mem_kv_kernel(mem_ref, g_ref, w_ref, o_ref, a_ref):
    @pl.when(pl.program_id(1) == 0)
    def _norm():
        a_ref[...] = (_rms(mem_ref[...]) * g_ref[...]).astype(a_ref.dtype)

    acc = jnp.dot(a_ref[...], w_ref[...].astype(BF16), preferred_element_type=F32)
    scale = jnp.where(pl.program_id(1) * 2 < pl.num_programs(1), MEM_DIM ** -0.5, 1.0)
    o_ref[...] = (acc * scale).astype(o_ref.dtype)


def _mem_kv(mem, gains, w, tn=1024):
    R, D = mem.shape
    L, _, N = w.shape
    return pl.pallas_call(
        _mem_kv_kernel, grid=(L, N // tn),
        in_specs=[pl.BlockSpec((R, D), lambda l, j: (0, 0)),
                  pl.BlockSpec((None, 1, D), lambda l, j: (l, 0, 0)),
                  pl.BlockSpec((None, D, tn), lambda l, j: (l, 0, j))],
        out_specs=pl.BlockSpec((None, R, tn), lambda l, j: (l, 0, j)),
        out_shape=jax.ShapeDtypeStruct((L, R, N), BF16),
        scratch_shapes=[pltpu.VMEM((R, D), BF16)],
        compiler_params=_params(2), name="mem_kv",
    )(mem, gains[:, None, :], w)


def _swa_rope(x, cos, sin_signed):
    lane = lax.broadcasted_iota(jnp.int32, x.shape, 1)
    first_half = (lane & (SWA_DIM - 1)) < (ROT_DIM // 2)
    partner = jnp.where(first_half, pltpu.roll(x, LANES - ROT_DIM // 2, 1), pltpu.roll(x, ROT_DIM // 2, 1))
    return x * cos + partner * sin_signed


IN_CHUNK = 256


def _mem_mixer_kernel(y_ref, wq_ref, wg_ref, mk_ref, mv_ref, o_ref, wb_ref):
    D = MEM_DIM

    @pl.when(pl.program_id(1) == 0)
    def _cast_weights():
        wb_ref[:, :D] = wq_ref[...].astype(BF16)
        wb_ref[:, D:] = wg_ref[...].astype(BF16)

    def project(c):
        z = jnp.dot(y_ref[pl.ds(c * IN_CHUNK, IN_CHUNK), :], wb_ref[...], preferred_element_type=F32)
        return z[:, :D].astype(BF16), _silu(z[:, D:])

    n_chunks = y_ref.shape[0] // IN_CHUNK
    nxt = project(0)
    for c in range(n_chunks):
        q, sg = nxt
        s = lax.dot_general(q, mk_ref[...], NT_DIMS, preferred_element_type=F32)
        if c + 1 < n_chunks:
            nxt = project(c + 1)
        e = jnp.exp(s - jnp.max(s, axis=-1, keepdims=True))
        p = (e / jnp.sum(e, axis=-1, keepdims=True)).astype(BF16)
        o = jnp.dot(p, mv_ref[...], preferred_element_type=F32)
        o_ref[pl.ds(c * IN_CHUNK, IN_CHUNK), :] = (o * sg).astype(o_ref.dtype)


def _mem_mixer(y, w, layer, q_group, mkv, mkv_layer, batch, tm=4096):
    M, K = y.shape
    S = M // batch
    tm = min(tm, S)
    tiles_per_seq = S // tm
    Mm = mkv.shape[1] // batch
    H = MEM_HEADS

    def wcol(group):
        return pl.BlockSpec((None, K, MEM_DIM), lambda m, i: (layer, 0, group * H + m))

    def mem(off):
        return pl.BlockSpec((None, Mm, MEM_DIM), lambda m, i: (mkv_layer, i // tiles_per_seq, off * H + m))

    return pl.pallas_call(
        _mem_mixer_kernel, grid=(H, M // tm),
        in_specs=[pl.BlockSpec((tm, K), lambda m, i: (i, 0)), wcol(q_group), wcol(q_group + 1), mem(0), mem(1)],
        out_specs=pl.BlockSpec((tm, MEM_DIM), lambda m, i: (i, m)),
        out_shape=jax.ShapeDtypeStruct((M, MEM_WIDTH), BF16),
        scratch_shapes=[pltpu.VMEM((K, 2 * MEM_DIM), BF16)],
        compiler_params=_params(2), name="mem_mixer",
    )(y, w, w, mkv, mkv)


def _side_cast_specs(w_out, layer, n_steps, steps_per_row):
    rows, cols = w_out.shape[1:]
    slab = rows // n_steps
    assert slab * n_steps == rows and slab % 16 == 0, (rows, n_steps)
    return (pl.BlockSpec((None, slab, cols), lambda a, b: (layer, a * steps_per_row + b, 0)),
            pl.BlockSpec((slab, cols), lambda a, b: (a * steps_per_row + b, 0)),
            jax.ShapeDtypeStruct((rows, cols), BF16))


def _retention_kernel(y_ref, wq_ref, wk_ref, wv_ref, wg_ref, cos_ref, sin_ref, wo_ref, o_ref, wo_bf16_ref,
                      wb_ref, r_ref, dec_ref, xi_ref, zeta_ref, gc_ref, *, tiles_per_seq):
    C = RET_CHUNK
    D = RET_DIM
    h = pl.program_id(0)
    i = pl.program_id(1)
    n_chunks = y_ref.shape[0] // C
    k_scale = RET_DIM ** -0.5
    wo_bf16_ref[...] = wo_ref[...].astype(BF16)

    @pl.when(i == 0)
    def _init():
        def log_gamma(shape):
            return jnp.log(1.0 - jnp.exp2(-5.0 - jnp.full(shape, h, jnp.int32).astype(F32)))

        for c, w_ref in enumerate((wq_ref, wk_ref, wv_ref, wg_ref)):
            wb_ref[:, c * D:(c + 1) * D] = w_ref[...].astype(BF16)
        n = lax.broadcasted_iota(jnp.int32, (C, RET_DIM), 0).astype(F32)
        xi_ref[...] = jnp.exp(log_gamma((C, RET_DIM)) * (n + 1.0))
        zeta_ref[...] = jnp.exp(log_gamma((C, RET_DIM)) * (C - 1.0 - n)) * k_scale
        gc_ref[...] = jnp.exp(log_gamma(gc_ref.shape) * float(C))
        diff = (lax.broadcasted_iota(jnp.int32, (C, C), 0)
                - lax.broadcasted_iota(jnp.int32, (C, C), 1)).astype(F32)
        dec_ref[...] = jnp.where(diff >= 0, jnp.exp(log_gamma((C, C)) * jnp.maximum(diff, 0.0)) * k_scale, 0.0)

    @pl.when(i % tiles_per_seq == 0)
    def _reset_state():
        r_ref[...] = jnp.zeros_like(r_ref)

    half = D // 2

    def rope(x, cos, sin):
        x1, x2 = x[:, :half], x[:, half:]
        return jnp.concatenate([x1 * cos - x2 * sin, x2 * cos + x1 * sin], axis=-1)

    def project(c):
        rows = pl.ds(c * C, C)
        z = jnp.dot(y_ref[rows, :], wb_ref[...], preferred_element_type=F32)
        cos, sin = cos_ref[rows, :], sin_ref[rows, :]
        k = rope(z[:, D:2 * D], cos, sin)
        return (rope(z[:, :D], cos, sin).astype(BF16), k.astype(BF16), (k * zeta_ref[...]).astype(BF16),
                z[:, 2 * D:3 * D].astype(BF16), _silu(z[:, 3 * D:]).astype(BF16))

    r = r_ref[...]
    nxt = project(0)
    for c in range(n_chunks):
        q, k, kz, v, sg = nxt
        if c + 1 < n_chunks:
            nxt = project(c + 1)
        inner = lax.dot_general(q, k, NT_DIMS, preferred_element_type=F32)
        update = lax.dot_general(kz, v, TN_DIMS, preferred_element_type=F32)
        cross = jnp.dot(q, r.astype(BF16), preferred_element_type=F32)
        o = jnp.dot((inner * dec_ref[...]).astype(BF16), v, preferred_element_type=F32) + xi_ref[...] * cross
        r = r * gc_ref[0:1, :] + update
        o_ref[pl.ds(c * C, C), :] = (_rms(o) * sg.astype(F32)).astype(o_ref.dtype)
    r_ref[...] = r


def _retention(y, w, layer, cos, sin, w_out, batch, tm=2048):
    M, K = y.shape
    S = M // batch
    tm = min(tm, S)
    H = RET_HEADS
    wo_in, wo_out, wo_shape = _side_cast_specs(w_out, layer, H * (M // tm), M // tm)

    def wcol(group):
        return pl.BlockSpec((None, K, RET_DIM), lambda h, i: (layer, 0, group * H + h))

    tab = pl.BlockSpec((tm, LANES), lambda h, i: (i, 0))
    return pl.pallas_call(
        functools.partial(_retention_kernel, tiles_per_seq=S // tm),
        grid=(H, M // tm),
        in_specs=[pl.BlockSpec((tm, K), lambda h, i: (i, 0)), wcol(0), wcol(1), wcol(2), wcol(3), tab, tab, wo_in],
        out_specs=[pl.BlockSpec((tm, RET_DIM), lambda h, i: (i, h)), wo_out],
        out_shape=[jax.ShapeDtypeStruct((M, RET_WIDTH), BF16), wo_shape],
        scratch_shapes=[pltpu.VMEM((K, 4 * RET_DIM), BF16),
                        pltpu.VMEM((RET_DIM, RET_DIM), F32),
                        pltpu.VMEM((RET_CHUNK, RET_CHUNK), F32),
                        pltpu.VMEM((RET_CHUNK, RET_DIM), F32),
                        pltpu.VMEM((RET_CHUNK, RET_DIM), F32),
                        pltpu.VMEM((8, RET_DIM), F32)],
        compiler_params=_params(2), name="retention",
    )(y, w, w, w, w, cos, sin, w_out)


def _kv_kernel(y_ref, w_ref, cos_ref, sin_ref, k_ref, vt_ref, wb_ref):
    @pl.when(pl.program_id(0) == 0)
    def _cast_weight():
        wb_ref[...] = w_ref[...].astype(BF16)

    low = lax.broadcasted_iota(jnp.int32, (IN_CHUNK, LANES), 1) < SWA_DIM
    for r in range(0, y_ref.shape[0], IN_CHUNK):
        rows = pl.ds(r, IN_CHUNK)
        kv = jnp.dot(y_ref[rows, :], wb_ref[...], preferred_element_type=F32)
        cos, sin = cos_ref[rows, :], sin_ref[rows, :]
        for blk in range(SWA_KV_WIDTH // LANES):
            k = _swa_rope(kv[:, blk * LANES:(blk + 1) * LANES], cos, sin)
            swapped = pltpu.roll(k, SWA_DIM, 1)
            tiles = (jnp.where(low, k, 0.0), jnp.where(low, 0.0, swapped),
                     jnp.where(low, swapped, 0.0), jnp.where(low, 0.0, k))
            for i, t in enumerate(tiles):
                k_ref[rows, (4 * blk + i) * LANES:(4 * blk + i + 1) * LANES] = t.astype(k_ref.dtype)
        vt_ref[:, r:r + IN_CHUNK] = kv[:, SWA_KV_WIDTH:].T.astype(vt_ref.dtype)


SWA_K_COLS = 2 * LANES * SWA_KV_HEADS


def _shared_kv(y, w, cos, sin, tm=2048):
    M, D = y.shape
    tm = min(tm, M)
    tab = pl.BlockSpec((tm, LANES), lambda i: (i, 0))
    return pl.pallas_call(
        _kv_kernel, grid=(M // tm,),
        in_specs=[pl.BlockSpec((tm, D), lambda i: (i, 0)),
                  pl.BlockSpec((D, 2 * SWA_KV_WIDTH), lambda i: (0, 0)), tab, tab],
        out_specs=[pl.BlockSpec((tm, SWA_K_COLS), lambda i: (i, 0)),
                   pl.BlockSpec((SWA_KV_WIDTH, tm), lambda i: (0, i))],
        out_shape=[jax.ShapeDtypeStruct((M, SWA_K_COLS), BF16),
                   jax.ShapeDtypeStruct((SWA_KV_WIDTH, M), BF16)],
        scratch_shapes=[pltpu.VMEM((D, 2 * SWA_KV_WIDTH), BF16)],
        compiler_params=_params(1), name="shared_kv",
    )(y, w, cos, sin)


SWA_ONES_ROWS = 16


SWA_GROUP = SWA_WIDTH // SWA_KV_HEADS


def _swa_kernel(sink_ref, y_ref, wq_ref, wg_ref, cos_ref, sin_ref, kp_ref, kc_ref, vtp_ref, vtc_ref, wo_ref,
                o_ref, wo_bf16_ref, wb_ref, *, tiles_per_seq):
    wo_bf16_ref[...] = wo_ref[...].astype(BF16)
    W = WINDOW
    n_pairs = SWA_GROUP // LANES
    QN = n_pairs * W
    g = pl.program_id(0)
    i = pl.program_id(1)
    n_chunks = y_ref.shape[0] // IN_CHUNK
    blocks_per_chunk = IN_CHUNK // W

    @pl.when(i == 0)
    def _cast_weights():
        wb_ref[:, :SWA_GROUP] = wq_ref[...].astype(BF16)
        wb_ref[:, SWA_GROUP:] = wg_ref[...].astype(BF16)

    H = W // 2
    QH = n_pairs * H
    jrow = lax.broadcasted_iota(jnp.int32, (H, QH), 0)
    icol = lax.broadcasted_iota(jnp.int32, (H, QH), 1)
    upper = jrow > (icol & (H - 1))
    upper_bf16 = jnp.where(upper, 1.0, 0.0).astype(BF16)
    pair_of_lane = lax.broadcasted_iota(jnp.int32, (1, QH), 1) // H
    ones_rows = jnp.ones((SWA_ONES_ROWS, W), BF16)
    no_keys = jnp.zeros((H, QH), BF16)
    no_prev_bias = jnp.where(i % tiles_per_seq == 0, MASK_VALUE, 0.0)

    def project(c):
        rows = pl.ds(c * IN_CHUNK, IN_CHUNK)
        z = jnp.dot(y_ref[rows, :], wb_ref[...], preferred_element_type=F32)
        cos, sin = cos_ref[rows, :], sin_ref[rows, :]
        q = [_swa_rope(z[:, j * LANES:(j + 1) * LANES], cos, sin).astype(BF16) for j in range(n_pairs)]
        sg = _silu(z[:, SWA_GROUP:]).astype(BF16)
        return [(jnp.concatenate([qj[b * W:b * W + H] for qj in q], axis=0),
                 jnp.concatenate([qj[b * W + H:(b + 1) * W] for qj in q], axis=0), sg[b * W:(b + 1) * W])
                for b in range(blocks_per_chunk)]

    def scores(t, q_early, q_late):
        early, late = [], []
        for parity in range(2):
            kcols = slice(parity * LANES, (parity + 1) * LANES)
            if t == 0:
                prev, prev_hi = kp_ref[:, kcols], kp_ref[pl.ds(H, H), kcols]
            else:
                prev, prev_hi = kc_ref[pl.ds((t - 1) * W, W), kcols], kc_ref[pl.ds((t - 1) * W + H, H), kcols]
            early += [prev, kc_ref[pl.ds(t * W, H), kcols]]
            late += [prev_hi, kc_ref[pl.ds(t * W, W), kcols]]
        return (lax.dot_general(jnp.concatenate(early, axis=0), q_early, NT_DIMS, preferred_element_type=F32),
                lax.dot_general(jnp.concatenate(late, axis=0), q_late, NT_DIMS, preferred_element_type=F32))

    def softmax(t, st):
        out = []
        for parity in range(2):
            halves = []
            for late in range(2):
                sp = st[late][parity * 3 * H:(parity + 1) * 3 * H]
                s_fold_prev = sp[:H] + no_prev_bias if t == 0 else sp[:H]
                s_full = sp[H:2 * H] + no_prev_bias if (t == 0 and not late) else sp[H:2 * H]
                s = jnp.concatenate([jnp.where(upper, s_fold_prev, sp[2 * H:]), s_full], axis=0)
                sink = jnp.full((1, QH), sink_ref[2 * n_pairs * g + 2 * (n_pairs - 1) + parity], F32)
                for j in range(n_pairs - 1):
                    sink = jnp.where(pair_of_lane == j, sink_ref[2 * n_pairs * g + 2 * j + parity], sink)
                sink = sink * LOG2_E
                m = jnp.maximum(jnp.max(s, axis=0, keepdims=True), sink)
                e = jnp.exp2(s - m).astype(BF16)
                e_prev = e[:H] * upper_bf16
                e_own = e[:H] - e_prev
                rows = ([no_keys, e_prev, e[H:], e_own] if late else [e_prev, e[H:], e_own, no_keys])
                halves.append((jnp.concatenate(rows, axis=0), jnp.exp2(sink - m)))
            out.append(halves)
        return out

    def finish(t, probs, sg):
        vt_prev = vtp_ref[...] if t == 0 else vtc_ref[:, (t - 1) * W:t * W]
        vt = jnp.concatenate([jnp.concatenate([vt_prev, ones_rows], axis=0),
                              jnp.concatenate([vtc_ref[:, t * W:(t + 1) * W], ones_rows], axis=0)], axis=1)
        ot = []
        for halves in probs:
            ot.append([])
            for e, sink_term in halves:
                acc = jnp.dot(vt, e, preferred_element_type=F32)
                ot[-1].append(acc[:SWA_DIM] / (acc[SWA_DIM:SWA_DIM + 1] + sink_term))
        for j in range(n_pairs):
            cols = slice(j * LANES, (j + 1) * LANES)
            qs = slice(j * H, (j + 1) * H)
            o = jnp.concatenate([jnp.concatenate([ot[parity][0][:, qs], ot[parity][1][:, qs]], axis=1)
                                 for parity in range(2)], axis=0).T
            o_ref[pl.ds(t * W, W), cols] = (o * sg[:, cols].astype(F32)).astype(o_ref.dtype)

    blocks = project(0)
    for c in range(n_chunks):
        ts = [c * blocks_per_chunk + b for b in range(blocks_per_chunk)]
        sts = [scores(t, q_early, q_late) for t, (q_early, q_late, _) in zip(ts, blocks)]
        nxt = project(c + 1) if c + 1 < n_chunks else None
        probs = [softmax(t, st) for t, st in zip(ts, sts)]
        for t, p, (_, _, sg) in zip(ts, probs, blocks):
            finish(t, p, sg)
        blocks = nxt


def _swa(y, w, layer, cos, sin, k2, vt, sinks, w_out, out_layer, batch, tm=2048):
    M, K = y.shape
    S = M // batch
    tm = min(tm, S)
    n_groups = SWA_KV_HEADS
    blocks_per_tile = tm // WINDOW
    wo_in, wo_out, wo_shape = _side_cast_specs(w_out, out_layer, n_groups * (M // tm), M // tm)

    def before(i):
        return jnp.maximum(i * blocks_per_tile - 1, 0)

    tab = pl.BlockSpec((tm, LANES), lambda g, i: (i, 0))
    return pl.pallas_call(
        functools.partial(_swa_kernel, tiles_per_seq=S // tm),
        grid=(n_groups, M // tm),
        in_specs=[pl.BlockSpec(memory_space=pltpu.SMEM),
                  pl.BlockSpec((tm, K), lambda g, i: (i, 0)),
                  pl.BlockSpec((None, K, SWA_GROUP), lambda g, i: (layer, 0, g)),
                  pl.BlockSpec((None, K, SWA_GROUP), lambda g, i: (layer, 0, n_groups + g)),
                  tab, tab,
                  pl.BlockSpec((WINDOW, 2 * LANES), lambda g, i: (before(i), g)),
                  pl.BlockSpec((tm, 2 * LANES), lambda g, i: (i, g)),
                  pl.BlockSpec((SWA_DIM, WINDOW), lambda g, i: (g, before(i))),
                  pl.BlockSpec((SWA_DIM, tm), lambda g, i: (g, i)), wo_in],
        out_specs=[pl.BlockSpec((tm, SWA_GROUP), lambda g, i: (i, g)), wo_out],
        out_shape=[jax.ShapeDtypeStruct((M, SWA_WIDTH), BF16), wo_shape],
        scratch_shapes=[pltpu.VMEM((K, 2 * SWA_GROUP), BF16)],
        compiler_params=_params(2), name="swa",
    )(sinks, y, w, w, cos, sin, k2, k2, vt, vt, w_out)


OUT_CHUNK = 128


def _out_kernel(a1_ref, a2_ref, w1_ref, w2_ref, h_ref, pg_ref, *rest, n_next):
    ng_ref = rest[0] if n_next else None
    h_out_ref = rest[1] if n_next else rest[0]
    y_refs = rest[2:2 + n_next]
    for r in range(0, a1_ref.shape[0], OUT_CHUNK):
        rows = pl.ds(r, OUT_CHUNK)
        acc = (jnp.dot(a1_ref[rows, :], w1_ref[...], preferred_element_type=F32)
               + jnp.dot(a2_ref[rows, :], w2_ref[...], preferred_element_type=F32))
        h = h_ref[rows, :] + _rms(acc) * pg_ref[...]
        h_out_ref[rows, :] = h
        if n_next:
            hn = _rms(h)
            for g in range(n_next):
                y_refs[g][rows, :] = (hn * ng_ref[g]).astype(y_refs[g].dtype)


def _mix_out(o_main, o_mem, w_out, h, post_g, next_g, tm=512):
    M = h.shape[0]
    tm = min(tm, M)
    n_next = len(next_g)
    k1, k2 = o_main.shape[1], o_mem.shape[1]
    assert k1 % k2 == 0

    def row(width):
        return pl.BlockSpec((tm, width), lambda i: (i, 0))

    once = pl.Buffered(1)
    in_specs = [row(k1), row(k2),
                pl.BlockSpec((k1, D_MODEL), lambda i: (0, 0), pipeline_mode=once),
                pl.BlockSpec((k2, D_MODEL), lambda i: (k1 // k2, 0), pipeline_mode=once),
                row(D_MODEL),
                pl.BlockSpec((1, D_MODEL), lambda i: (0, 0))]
    args = [o_main, o_mem, w_out, w_out, h, post_g[None]]
    if n_next:
        in_specs.append(pl.BlockSpec((n_next, 1, D_MODEL), lambda i: (0, 0, 0)))
        args.append(jnp.stack(next_g)[:, None, :])
    out_specs = [row(D_MODEL)] * (1 + n_next)
    out_shape = [jax.ShapeDtypeStruct((M, D_MODEL), F32)] + [jax.ShapeDtypeStruct((M, D_MODEL), BF16)] * n_next
    res = pl.pallas_call(
        functools.partial(_out_kernel, n_next=n_next), grid=(M // tm,),
        in_specs=in_specs, out_specs=out_specs, out_shape=out_shape,
        compiler_params=_params(1), name="mix_out",
    )(*args)
    return res[0], res[1:]


def kernel(x, mem, positions, pre_norm_g, post_norm_g, mem_norm_g, kv_norm_g,
           w_in_a, w_in_b, w_kv_b, sinks_b, w_mem_kv, w_out):
    B, S, D = x.shape
    Mm = mem.shape[1]
    M = B * S
    h = x.reshape(M, D)
    y, cos_r, sin_r, cos_k, sin_k, cos_q, sin_q = _prologue(h, pre_norm_g[0], positions.astype(F32).reshape(M, 1))
    mkv = _mem_kv(mem.reshape(B * Mm, D), mem_norm_g, w_mem_kv)
    k2 = vt = None
    for l in range(DEPTH):
        if l < N_A:
            o_mem = _mem_mixer(y, w_in_a, l, MEM_GROUP_A, mkv, l, B)
            o_main, w_out_l = _retention(y, w_in_a, l, cos_r, sin_r, w_out, B)
        else:
            o_mem = _mem_mixer(y, w_in_b, l - N_A, MEM_GROUP_B, mkv, l, B)
            o_main, w_out_l = _swa(y, w_in_b, l - N_A, cos_q, sin_q, k2, vt, sinks_b[l - N_A], w_out, l, B)
        next_g = [pre_norm_g[l + 1]] if l < DEPTH - 1 else []
        if l == N_A - 1:
            next_g.append(kv_norm_g)
        h, ys = _mix_out(o_main, o_mem, w_out_l, h, post_norm_g[l], next_g)
        if ys:
            y = ys[0]
        if l == N_A - 1:
            k2, vt = _shared_kv(ys[1], w_kv_b, cos_k, sin_k)
    return h.reshape(B, S, D)
```

```python
import functools

import jax
import jax.numpy as jnp
from jax import lax
from jax.experimental import pallas as pl
from jax.experimental.pallas import tpu as pltpu

F32 = jnp.float32
BF16 = jnp.bfloat16

D_MODEL = 2048
DEPTH = 4
N_A = DEPTH // 2
RET_DIM = 256
RET_HEADS = D_MODEL // RET_DIM
RET_WIDTH = RET_HEADS * RET_DIM
RET_CHUNK = 256
RET_THETA = 10000.0
SWA_DIM = 64
SWA_HEADS = D_MODEL // SWA_DIM
SWA_KV_HEADS = SWA_HEADS // 8
SWA_WIDTH = SWA_HEADS * SWA_DIM
SWA_KV_WIDTH = SWA_KV_HEADS * SWA_DIM
WINDOW = 128
ROPE_THETA = 500000.0
ROT_DIM = SWA_DIM // 4
MEM_HEADS = 4
MEM_DIM = D_MODEL // 8
MEM_WIDTH = MEM_HEADS * MEM_DIM
EPS = 1e-6
MASK_VALUE = -1e30
LOG2_E = 1.4426950408889634

LANES = 128
VMEM_LIMIT = 60 * 1024 * 1024

NT_DIMS = (((1,), (1,)), ((), ()))
TN_DIMS = (((0,), (0,)), ((), ()))

MEM_GROUP_A = 4 * RET_WIDTH // MEM_WIDTH
MEM_GROUP_B = 2 * SWA_WIDTH // MEM_WIDTH


def _params(n_axes):
    return pltpu.CompilerParams(dimension_semantics=("arbitrary",) * n_axes,
                                vmem_limit_bytes=VMEM_LIMIT)


def _silu(g):
    return g * (1.0 / (1.0 + jnp.exp(-g)))


def _rms(x):
    return x * lax.rsqrt(jnp.mean(x * x, axis=-1, keepdims=True) + EPS)


def _prologue_kernel(x_ref, g_ref, pos_ref, inv_r_ref, inv_s_ref, sgn_ref,
                     y_ref, cr_ref, sr_ref, ck_ref, sk_ref, cq_ref, sq_ref):
    y_ref[...] = (_rms(x_ref[...]) * g_ref[...]).astype(y_ref.dtype)
    pos = pos_ref[...]
    ang_r = pos * inv_r_ref[...]
    cr_ref[...] = jnp.cos(ang_r)
    sr_ref[...] = jnp.sin(ang_r)
    ang_s = pos * inv_s_ref[...]
    groups = SWA_DIM // ROT_DIM
    rows = ang_s.shape[0] // groups
    packed = ang_s[:rows]
    for b in range(1, groups):
        packed = packed + pltpu.roll(ang_s[b * rows:(b + 1) * rows], b * ROT_DIM, 1)
    cos_p, sin_p = jnp.cos(packed), jnp.sin(packed)
    rotary_lane = (lax.broadcasted_iota(jnp.int32, packed.shape, 1) & (SWA_DIM - 1)) < ROT_DIM
    for b in range(groups):
        unroll = (lambda t: pltpu.roll(t, LANES - b * ROT_DIM, 1)) if b else (lambda t: t)
        cos_s = jnp.where(rotary_lane, unroll(cos_p), 1.0)
        sin_s = unroll(sin_p) * sgn_ref[...]
        part = pl.ds(b * rows, rows)
        ck_ref[part, :] = cos_s
        sk_ref[part, :] = sin_s
        cq_ref[part, :] = cos_s * (LOG2_E * SWA_DIM ** -0.5)
        sq_ref[part, :] = sin_s * (LOG2_E * SWA_DIM ** -0.5)


def _prologue(x, gain, pos, tm=1024):
    M, D = x.shape
    tm = min(tm, M)
    inv_r = RET_THETA ** (-jnp.arange(0, RET_DIM, 2, dtype=F32) / RET_DIM)
    inv_h = ROPE_THETA ** (-jnp.arange(0, ROT_DIM, 2, dtype=F32) / ROT_DIM)
    half = ROT_DIM // 2
    head = jnp.concatenate([inv_h, inv_h, jnp.zeros((SWA_DIM - ROT_DIM,), F32)])
    sgn_h = jnp.concatenate([-jnp.ones((half,), F32), jnp.ones((half,), F32),
                             jnp.zeros((SWA_DIM - ROT_DIM,), F32)])
    inv_s = jnp.tile(head, LANES // SWA_DIM)
    sgn = jnp.tile(sgn_h, LANES // SWA_DIM)
    row = pl.BlockSpec((1, LANES), lambda i: (0, 0))
    tab = pl.BlockSpec((tm, LANES), lambda i: (i, 0))
    shape = jax.ShapeDtypeStruct((M, LANES), F32)
    wide = pl.BlockSpec((tm, D), lambda i: (i, 0))
    return pl.pallas_call(
        _prologue_kernel, grid=(M // tm,),
        in_specs=[wide, pl.BlockSpec((1, D), lambda i: (0, 0)), pl.BlockSpec((tm, 1), lambda i: (i, 0)),
                  row, row, row],
        out_specs=[wide] + [tab] * 6,
        out_shape=[jax.ShapeDtypeStruct((M, D), BF16)] + [shape] * 6,
        compiler_params=_params(1), name="prologue",
    )(x, gain[None], pos, inv_r[None], inv_s[None], sgn[None])


def _mem_kv_kernel(mem_ref, g_ref, w_ref, o_ref, a_ref):
    @pl.when(pl.program_id(1) == 0)
    def _norm():
        a_ref[...] = (_rms(mem_ref[...]) * g_ref[...]).astype(a_ref.dtype)

    acc = jnp.dot(a_ref[...], w_ref[...].astype(BF16), preferred_element_type=F32)
    scale = jnp.where(pl.program_id(1) * 2 < pl.num_programs(1), MEM_DIM ** -0.5, 1.0)
    o_ref[...] = (acc * scale).astype(o_ref.dtype)


def _mem_kv(mem, gains, w, tn=1024):
    R, D = mem.shape
    L, _, N = w.shape
    return pl.pallas_call(
        _mem_kv_kernel, grid=(L, N // tn),
        in_specs=[pl.BlockSpec((R, D), lambda l, j: (0, 0)),
                  pl.BlockSpec((None, 1, D), lambda l, j: (l, 0, 0)),
                  pl.BlockSpec((None, D, tn), lambda l, j: (l, 0, j))],
        out_specs=pl.BlockSpec((None, R, tn), lambda l, j: (l, 0, j)),
        out_shape=jax.ShapeDtypeStruct((L, R, N), BF16),
        scratch_shapes=[pltpu.VMEM((R, D), BF16)],
        compiler_params=_params(2), name="mem_kv",
    )(mem, gains[:, None, :], w)


def _swa_rope(x, cos, sin_signed):
    lane = lax.broadcasted_iota(jnp.int32, x.shape, 1)
    first_half = (lane & (SWA_DIM - 1)) < (ROT_DIM // 2)
    partner = jnp.where(first_half, pltpu.roll(x, LANES - ROT_DIM // 2, 1), pltpu.roll(x, ROT_DIM // 2, 1))
    return x * cos + partner * sin_signed


IN_CHUNK = 256


def _mem_mixer_kernel(y_ref, wq_ref, wg_ref, mk_ref, mv_ref, o_ref, wb_ref):
    D = MEM_DIM

    @pl.when(pl.program_id(1) == 0)
    def _cast_weights():
        wb_ref[:, :D] = wq_ref[...].astype(BF16)
        wb_ref[:, D:] = wg_ref[...].astype(BF16)

    def project(c):
        z = jnp.dot(y_ref[pl.ds(c * IN_CHUNK, IN_CHUNK), :], wb_ref[...], preferred_element_type=F32)
        return z[:, :D].astype(BF16), _silu(z[:, D:])

    n_chunks = y_ref.shape[0] // IN_CHUNK
    nxt = project(0)
    for c in range(n_chunks):
        q, sg = nxt
        s = lax.dot_general(q, mk_ref[...], NT_DIMS, preferred_element_type=F32)
        if c + 1 < n_chunks:
            nxt = project(c + 1)
        e = jnp.exp(s - jnp.max(s, axis=-1, keepdims=True))
        p = (e / jnp.sum(e, axis=-1, keepdims=True)).astype(BF16)
        o = jnp.dot(p, mv_ref[...], preferred_element_type=F32)
        o_ref[pl.ds(c * IN_CHUNK, IN_CHUNK), :] = (o * sg).astype(o_ref.dtype)


def _mem_mixer(y, w, layer, q_group, mkv, mkv_layer, batch, tm=4096):
    M, K = y.shape
    S = M // batch
    tm = min(tm, S)
    tiles_per_seq = S // tm
    Mm = mkv.shape[1] // batch
    H = MEM_HEADS

    def wcol(group):
        return pl.BlockSpec((None, K, MEM_DIM), lambda m, i: (layer, 0, group * H + m))

    def mem(off):
        return pl.BlockSpec((None, Mm, MEM_DIM), lambda m, i: (mkv_layer, i // tiles_per_seq, off * H + m))

    return pl.pallas_call(
        _mem_mixer_kernel, grid=(H, M // tm),
        in_specs=[pl.BlockSpec((tm, K), lambda m, i: (i, 0)), wcol(q_group), wcol(q_group + 1), mem(0), mem(1)],
        out_specs=pl.BlockSpec((tm, MEM_DIM), lambda m, i: (i, m)),
        out_shape=jax.ShapeDtypeStruct((M, MEM_WIDTH), BF16),
        scratch_shapes=[pltpu.VMEM((K, 2 * MEM_DIM), BF16)],
        compiler_params=_params(2), name="mem_mixer",
    )(y, w, w, mkv, mkv)


def _side_cast_specs(w_out, layer, n_steps, steps_per_row):
    rows, cols = w_out.shape[1:]
    slab = rows // n_steps
    assert slab * n_steps == rows and slab % 16 == 0, (rows, n_steps)
    return (pl.BlockSpec((None, slab, cols), lambda a, b: (layer, a * steps_per_row + b, 0)),
            pl.BlockSpec((slab, cols), lambda a, b: (a * steps_per_row + b, 0)),
            jax.ShapeDtypeStruct((rows, cols), BF16))


def _retention_kernel(y_ref, wq_ref, wk_ref, wv_ref, wg_ref, cos_ref, sin_ref, wo_ref, o_ref, wo_bf16_ref,
                      wb_ref, r_ref, dec_ref, xi_ref, zeta_ref, gc_ref, *, tiles_per_seq):
    C = RET_CHUNK
    D = RET_DIM
    h = pl.program_id(0)
    i = pl.program_id(1)
    n_chunks = y_ref.shape[0] // C
    k_scale = RET_DIM ** -0.5
    wo_bf16_ref[...] = wo_ref[...].astype(BF16)

    @pl.when(i == 0)
    def _init():
        def log_gamma(shape):
            return jnp.log(1.0 - jnp.exp2(-5.0 - jnp.full(shape, h, jnp.int32).astype(F32)))

        for c, w_ref in enumerate((wq_ref, wk_ref, wv_ref, wg_ref)):
            wb_ref[:, c * D:(c + 1) * D] = w_ref[...].astype(BF16)
        n = lax.broadcasted_iota(jnp.int32, (C, RET_DIM), 0).astype(F32)
        xi_ref[...] = jnp.exp(log_gamma((C, RET_DIM)) * (n + 1.0))
        zeta_ref[...] = jnp.exp(log_gamma((C, RET_DIM)) * (C - 1.0 - n)) * k_scale
        gc_ref[...] = jnp.exp(log_gamma(gc_ref.shape) * float(C))
        diff = (lax.broadcasted_iota(jnp.int32, (C, C), 0)
                - lax.broadcasted_iota(jnp.int32, (C, C), 1)).astype(F32)
        dec_ref[...] = jnp.where(diff >= 0, jnp.exp(log_gamma((C, C)) * jnp.maximum(diff, 0.0)) * k_scale, 0.0)

    @pl.when(i % tiles_per_seq == 0)
    def _reset_state():
        r_ref[...] = jnp.zeros_like(r_ref)

    half = D // 2

    def rope(x, cos, sin):
        x1, x2 = x[:, :half], x[:, half:]
        return jnp.concatenate([x1 * cos - x2 * sin, x2 * cos + x1 * sin], axis=-1)

    def project(c):
        rows = pl.ds(c * C, C)
        z = jnp.dot(y_ref[rows, :], wb_ref[...], preferred_element_type=F32)
        cos, sin = cos_ref[rows, :], sin_ref[rows, :]
        k = rope(z[:, D:2 * D], cos, sin)
        return (rope(z[:, :D], cos, sin).astype(BF16), k.astype(BF16), (k * zeta_ref[...]).astype(BF16),
                z[:, 2 * D:3 * D].astype(BF16), _silu(z[:, 3 * D:]).astype(BF16))

    r = r_ref[...]
    nxt = project(0)
    for c in range(n_chunks):
        q, k, kz, v, sg = nxt
        if c + 1 < n_chunks:
            nxt = project(c + 1)
        inner = lax.dot_general(q, k, NT_DIMS, preferred_element_type=F32)
        update = lax.dot_general(kz, v, TN_DIMS, preferred_element_type=F32)
        cross = jnp.dot(q, r.astype(BF16), preferred_element_type=F32)
        o = jnp.dot((inner * dec_ref[...]).astype(BF16), v, preferred_element_type=F32) + xi_ref[...] * cross
        r = r * gc_ref[0:1, :] + update
        o_ref[pl.ds(c * C, C), :] = (_rms(o) * sg.astype(F32)).astype(o_ref.dtype)
    r_ref[...] = r


def _retention(y, w, layer, cos, sin, w_out, batch, tm=2048):
    M, K = y.shape
    S = M // batch
    tm = min(tm, S)
    H = RET_HEADS
    wo_in, wo_out, wo_shape = _side_cast_specs(w_out, layer, H * (M // tm), M // tm)

    def wcol(group):
        return pl.BlockSpec((None, K, RET_DIM), lambda h, i: (layer, 0, group * H + h))

    tab = pl.BlockSpec((tm, LANES), lambda h, i: (i, 0))
    return pl.pallas_call(
        functools.partial(_retention_kernel, tiles_per_seq=S // tm),
        grid=(H, M // tm),
        in_specs=[pl.BlockSpec((tm, K), lambda h, i: (i, 0)), wcol(0), wcol(1), wcol(2), wcol(3), tab, tab, wo_in],
        out_specs=[pl.BlockSpec((tm, RET_DIM), lambda h, i: (i, h)), wo_out],
        out_shape=[jax.ShapeDtypeStruct((M, RET_WIDTH), BF16), wo_shape],
        scratch_shapes=[pltpu.VMEM((K, 4 * RET_DIM), BF16),
                        pltpu.VMEM((RET_DIM, RET_DIM), F32),
                        pltpu.VMEM((RET_CHUNK, RET_CHUNK), F32),
                        pltpu.VMEM((RET_CHUNK, RET_DIM), F32),
                        pltpu.VMEM((RET_CHUNK, RET_DIM), F32),
                        pltpu.VMEM((8, RET_DIM), F32)],
        compiler_params=_params(2), name="retention",
    )(y, w, w, w, w, cos, sin, w_out)


def _kv_kernel(y_ref, w_ref, cos_ref, sin_ref, k_ref, vt_ref, wb_ref):
    @pl.when(pl.program_id(0) == 0)
    def _cast_weight():
        wb_ref[...] = w_ref[...].astype(BF16)

    low = lax.broadcasted_iota(jnp.int32, (IN_CHUNK, LANES), 1) < SWA_DIM
    for r in range(0, y_ref.shape[0], IN_CHUNK):
        rows = pl.ds(r, IN_CHUNK)
        kv = jnp.dot(y_ref[rows, :], wb_ref[...], preferred_element_type=F32)
        cos, sin = cos_ref[rows, :], sin_ref[rows, :]
        for blk in range(SWA_KV_WIDTH // LANES):
            k = _swa_rope(kv[:, blk * LANES:(blk + 1) * LANES], cos, sin)
            swapped = pltpu.roll(k, SWA_DIM, 1)
            tiles = (jnp.where(low, k, 0.0), jnp.where(low, 0.0, swapped),
                     jnp.where(low, swapped, 0.0), jnp.where(low, 0.0, k))
            for i, t in enumerate(tiles):
                k_ref[rows, (4 * blk + i) * LANES:(4 * blk + i + 1) * LANES] = t.astype(k_ref.dtype)
        vt_ref[:, r:r + IN_CHUNK] = kv[:, SWA_KV_WIDTH:].T.astype(vt_ref.dtype)


SWA_K_COLS = 2 * LANES * SWA_KV_HEADS


def _shared_kv(y, w, cos, sin, tm=2048):
    M, D = y.shape
    tm = min(tm, M)
    tab = pl.BlockSpec((tm, LANES), lambda i: (i, 0))
    return pl.pallas_call(
        _kv_kernel, grid=(M // tm,),
        in_specs=[pl.BlockSpec((tm, D), lambda i: (i, 0)),
                  pl.BlockSpec((D, 2 * SWA_KV_WIDTH), lambda i: (0, 0)), tab, tab],
        out_specs=[pl.BlockSpec((tm, SWA_K_COLS), lambda i: (i, 0)),
                   pl.BlockSpec((SWA_KV_WIDTH, tm), lambda i: (0, i))],
        out_shape=[jax.ShapeDtypeStruct((M, SWA_K_COLS), BF16),
                   jax.ShapeDtypeStruct((SWA_KV_WIDTH, M), BF16)],
        scratch_shapes=[pltpu.VMEM((D, 2 * SWA_KV_WIDTH), BF16)],
        compiler_params=_params(1), name="shared_kv",
    )(y, w, cos, sin)


SWA_ONES_ROWS = 16


SWA_GROUP = SWA_WIDTH // SWA_KV_HEADS


def _swa_kernel(sink_ref, y_ref, wq_ref, wg_ref, cos_ref, sin_ref, kp_ref, kc_ref, vtp_ref, vtc_ref, wo_ref,
                o_ref, wo_bf16_ref, wb_ref, *, tiles_per_seq):
    wo_bf16_ref[...] = wo_ref[...].astype(BF16)
    W = WINDOW
    n_pairs = SWA_GROUP // LANES
    QN = n_pairs * W
    g = pl.program_id(0)
    i = pl.program_id(1)
    n_chunks = y_ref.shape[0] // IN_CHUNK
    blocks_per_chunk = IN_CHUNK // W

    @pl.when(i == 0)
    def _cast_weights():
        wb_ref[:, :SWA_GROUP] = wq_ref[...].astype(BF16)
        wb_ref[:, SWA_GROUP:] = wg_ref[...].astype(BF16)

    H = W // 2
    QH = n_pairs * H
    jrow = lax.broadcasted_iota(jnp.int32, (H, QH), 0)
    icol = lax.broadcasted_iota(jnp.int32, (H, QH), 1)
    upper = jrow > (icol & (H - 1))
    upper_bf16 = jnp.where(upper, 1.0, 0.0).astype(BF16)
    pair_of_lane = lax.broadcasted_iota(jnp.int32, (1, QH), 1) // H
    ones_rows = jnp.ones((SWA_ONES_ROWS, W), BF16)
    no_keys = jnp.zeros((H, QH), BF16)
    no_prev_bias = jnp.where(i % tiles_per_seq == 0, MASK_VALUE, 0.0)

    def project(c):
        rows = pl.ds(c * IN_CHUNK, IN_CHUNK)
        z = jnp.dot(y_ref[rows, :], wb_ref[...], preferred_element_type=F32)
        cos, sin = cos_ref[rows, :], sin_ref[rows, :]
        q = [_swa_rope(z[:, j * LANES:(j + 1) * LANES], cos, sin).astype(BF16) for j in range(n_pairs)]
        sg = _silu(z[:, SWA_GROUP:]).astype(BF16)
        return [(jnp.concatenate([qj[b * W:b * W + H] for qj in q], axis=0),
                 jnp.concatenate([qj[b * W + H:(b + 1) * W] for qj in q], axis=0), sg[b * W:(b + 1) * W])
                for b in range(blocks_per_chunk)]

    def scores(t, q_early, q_late):
        early, late = [], []
        for parity in range(2):
            kcols = slice(parity * LANES, (parity + 1) * LANES)
            if t == 0:
                prev, prev_hi = kp_ref[:, kcols], kp_ref[pl.ds(H, H), kcols]
            else:
                prev, prev_hi = kc_ref[pl.ds((t - 1) * W, W), kcols], kc_ref[pl.ds((t - 1) * W + H, H), kcols]
            early += [prev, kc_ref[pl.ds(t * W, H), kcols]]
            late += [prev_hi, kc_ref[pl.ds(t * W, W), kcols]]
        return (lax.dot_general(jnp.concatenate(early, axis=0), q_early, NT_DIMS, preferred_element_type=F32),
                lax.dot_general(jnp.concatenate(late, axis=0), q_late, NT_DIMS, preferred_element_type=F32))

    def softmax(t, st):
        out = []
        for parity in range(2):
            halves = []
            for late in range(2):
                sp = st[late][parity * 3 * H:(parity + 1) * 3 * H]
                s_fold_prev = sp[:H] + no_prev_bias if t == 0 else sp[:H]
                s_full = sp[H:2 * H] + no_prev_bias if (t == 0 and not late) else sp[H:2 * H]
                s = jnp.concatenate([jnp.where(upper, s_fold_prev, sp[2 * H:]), s_full], axis=0)
                sink = jnp.full((1, QH), sink_ref[2 * n_pairs * g + 2 * (n_pairs - 1) + parity], F32)
                for j in range(n_pairs - 1):
                    sink = jnp.where(pair_of_lane == j, sink_ref[2 * n_pairs * g + 2 * j + parity], sink)
                sink = sink * LOG2_E
                m = jnp.maximum(jnp.max(s, axis=0, keepdims=True), sink)
                e = jnp.exp2(s - m).astype(BF16)
                e_prev = e[:H] * upper_bf16
                e_own = e[:H] - e_prev
                rows = ([no_keys, e_prev, e[H:], e_own] if late else [e_prev, e[H:], e_own, no_keys])
                halves.append((jnp.concatenate(rows, axis=0), jnp.exp2(sink - m)))
            out.append(halves)
        return out

    def finish(t, probs, sg):
        vt_prev = vtp_ref[...] if t == 0 else vtc_ref[:, (t - 1) * W:t * W]
        vt = jnp.concatenate([jnp.concatenate([vt_prev, ones_rows], axis=0),
                              jnp.concatenate([vtc_ref[:, t * W:(t + 1) * W], ones_rows], axis=0)], axis=1)
        ot = []
        for halves in probs:
            ot.append([])
            for e, sink_term in halves:
                acc = jnp.dot(vt, e, preferred_element_type=F32)
                ot[-1].append(acc[:SWA_DIM] / (acc[SWA_DIM:SWA_DIM + 1] + sink_term))
        for j in range(n_pairs):
            cols = slice(j * LANES, (j + 1) * LANES)
            qs = slice(j * H, (j + 1) * H)
            o = jnp.concatenate([jnp.concatenate([ot[parity][0][:, qs], ot[parity][1][:, qs]], axis=1)
                                 for parity in range(2)], axis=0).T
            o_ref[pl.ds(t * W, W), cols] = (o * sg[:, cols].astype(F32)).astype(o_ref.dtype)

    blocks = project(0)
    for c in range(n_chunks):
        ts = [c * blocks_per_chunk + b for b in range(blocks_per_chunk)]
        sts = [scores(t, q_early, q_late) for t, (q_early, q_late, _) in zip(ts, blocks)]
        nxt = project(c + 1) if c + 1 < n_chunks else None
        probs = [softmax(t, st) for t, st in zip(ts, sts)]
        for t, p, (_, _, sg) in zip(ts, probs, blocks):
            finish(t, p, sg)
        blocks = nxt


def _swa(y, w, layer, cos, sin, k2, vt, sinks, w_out, out_layer, batch, tm=2048):
    M, K = y.shape
    S = M // batch
    tm = min(tm, S)
    n_groups = SWA_KV_HEADS
    blocks_per_tile = tm // WINDOW
    wo_in, wo_out, wo_shape = _side_cast_specs(w_out, out_layer, n_groups * (M // tm), M // tm)

    def before(i):
        return jnp.maximum(i * blocks_per_tile - 1, 0)

    tab = pl.BlockSpec((tm, LANES), lambda g, i: (i, 0))
    return pl.pallas_call(
        functools.partial(_swa_kernel, tiles_per_seq=S // tm),
        grid=(n_groups, M // tm),
        in_specs=[pl.BlockSpec(memory_space=pltpu.SMEM),
                  pl.BlockSpec((tm, K), lambda g, i: (i, 0)),
                  pl.BlockSpec((None, K, SWA_GROUP), lambda g, i: (layer, 0, g)),
                  pl.BlockSpec((None, K, SWA_GROUP), lambda g, i: (layer, 0, n_groups + g)),
                  tab, tab,
                  pl.BlockSpec((WINDOW, 2 * LANES), lambda g, i: (before(i), g)),
                  pl.BlockSpec((tm, 2 * LANES), lambda g, i: (i, g)),
                  pl.BlockSpec((SWA_DIM, WINDOW), lambda g, i: (g, before(i))),
                  pl.BlockSpec((SWA_DIM, tm), lambda g, i: (g, i)), wo_in],
        out_specs=[pl.BlockSpec((tm, SWA_GROUP), lambda g, i: (i, g)), wo_out],
        out_shape=[jax.ShapeDtypeStruct((M, SWA_WIDTH), BF16), wo_shape],
        scratch_shapes=[pltpu.VMEM((K, 2 * SWA_GROUP), BF16)],
        compiler_params=_params(2), name="swa",
    )(sinks, y, w, w, cos, sin, k2, k2, vt, vt, w_out)


OUT_CHUNK = 128


def _out_kernel(a1_ref, a2_ref, w1_ref, w2_ref, h_ref, h_late_ref, pg_ref, *rest, n_next, n_tiles):
    ng_ref = rest[0] if n_next else None
    rest = rest[1:] if n_next else rest
    n_out = 1 + n_next
    outs, stages = rest[:n_out], rest[n_out:2 * n_out]
    carry_ref, sem = rest[2 * n_out], rest[2 * n_out + 1]
    tm, C = a1_ref.shape[0], OUT_CHUNK
    i = pl.program_id(0)
    slot = i % 2

    def copies(kind, step, slot_):
        src0, rows, dst0 = {"first": (C, tm - C, 0), "mid": (0, tm, step * tm - C),
                            "last": (0, C, n_tiles * tm - C)}[kind]
        return [pltpu.make_async_copy(stages[k].at[slot_, pl.ds(src0, rows)], outs[k].at[pl.ds(dst0, rows)],
                                      sem.at[k, slot_]) for k in range(n_out)]

    def start(kind):
        for cp in copies(kind, i, slot):
            cp.start()

    def wait(kind, step):
        for cp in copies(kind, step, step % 2):
            cp.wait()

    def project(c):
        rows = pl.ds(c * C, C)
        return (jnp.dot(a1_ref[rows, :], w1_ref[...], preferred_element_type=F32)
                + jnp.dot(a2_ref[rows, :], w2_ref[...], preferred_element_type=F32))

    def epilogue(acc, h_rows, dst):
        h = h_rows + _rms(acc) * pg_ref[...]
        stages[0][slot, pl.ds(dst, C), :] = h
        if n_next:
            hn = _rms(h)
            for g in range(n_next):
                stages[1 + g][slot, pl.ds(dst, C), :] = (hn * ng_ref[g]).astype(stages[1 + g].dtype)

    @pl.when(i == 0)
    def _no_carry_yet():
        carry_ref[...] = jnp.zeros_like(carry_ref)

    pl.when(i == 2)(lambda: wait("first", 0))
    pl.when(i > 2)(lambda: wait("mid", i - 2))

    @pl.when(i < n_tiles)
    def _tile():
        n_chunks = tm // C
        acc = project(0)
        epilogue(carry_ref[...], h_late_ref[...], 0)
        for c in range(n_chunks - 1):
            nxt = project(c + 1)
            epilogue(acc, h_ref[pl.ds(c * C, C), :], (c + 1) * C)
            acc = nxt
        carry_ref[...] = acc

    pl.when(i == 0)(lambda: start("first"))
    pl.when((i > 0) & (i < n_tiles))(lambda: start("mid"))

    @pl.when(i == n_tiles)
    def _drain():
        epilogue(carry_ref[...], h_late_ref[...], 0)
        start("last")
        wait("mid", n_tiles - 1)
        wait("last", n_tiles)


def _mix_out(o_main, o_mem, w_out, h, post_g, next_g, tm=512):
    M = h.shape[0]
    tm = min(tm, M)
    n_tiles = M // tm
    assert n_tiles >= 2 and tm % OUT_CHUNK == 0
    n_next = len(next_g)
    k1, k2 = o_main.shape[1], o_mem.shape[1]
    assert k1 % k2 == 0
    chunks_per_tile = tm // OUT_CHUNK

    def row(width):
        return pl.BlockSpec((tm, width), lambda i: (jnp.minimum(i, n_tiles - 1), 0))

    once = pl.Buffered(1)
    in_specs = [row(k1), row(k2),
                pl.BlockSpec((k1, D_MODEL), lambda i: (0, 0), pipeline_mode=once),
                pl.BlockSpec((k2, D_MODEL), lambda i: (k1 // k2, 0), pipeline_mode=once),
                row(D_MODEL),
                pl.BlockSpec((OUT_CHUNK, D_MODEL), lambda i: (jnp.maximum(i * chunks_per_tile - 1, 0), 0)),
                pl.BlockSpec((1, D_MODEL), lambda i: (0, 0))]
    args = [o_main, o_mem, w_out, w_out, h, h, post_g[None]]
    if n_next:
        in_specs.append(pl.BlockSpec((n_next, 1, D_MODEL), lambda i: (0, 0, 0)))
        args.append(jnp.stack(next_g)[:, None, :])
    dtypes = [F32] + [BF16] * n_next
    res = pl.pallas_call(
        functools.partial(_out_kernel, n_next=n_next, n_tiles=n_tiles), grid=(n_tiles + 1,),
        in_specs=in_specs,
        out_specs=[pl.BlockSpec(memory_space=pl.ANY)] * (1 + n_next),
        out_shape=[jax.ShapeDtypeStruct((M, D_MODEL), dt) for dt in dtypes],
        scratch_shapes=[pltpu.VMEM((2, tm, D_MODEL), dt) for dt in dtypes]
        + [pltpu.VMEM((OUT_CHUNK, D_MODEL), F32), pltpu.SemaphoreType.DMA((1 + n_next, 2))],
        compiler_params=_params(1), name="mix_out",
    )(*args)
    return res[0], res[1:]


def kernel(x, mem, positions, pre_norm_g, post_norm_g, mem_norm_g, kv_norm_g,
           w_in_a, w_in_b, w_kv_b, sinks_b, w_mem_kv, w_out):
    B, S, D = x.shape
    Mm = mem.shape[1]
    M = B * S
    h = x.reshape(M, D)
    y, cos_r, sin_r, cos_k, sin_k, cos_q, sin_q = _prologue(h, pre_norm_g[0], positions.astype(F32).reshape(M, 1))
    mkv = _mem_kv(mem.reshape(B * Mm, D), mem_norm_g, w_mem_kv)
    k2 = vt = None
    for l in range(DEPTH):
        if l < N_A:
            o_mem = _mem_mixer(y, w_in_a, l, MEM_GROUP_A, mkv, l, B)
            o_main, w_out_l = _retention(y, w_in_a, l, cos_r, sin_r, w_out, B)
        else:
            o_mem = _mem_mixer(y, w_in_b, l - N_A, MEM_GROUP_B, mkv, l, B)
            o_main, w_out_l = _swa(y, w_in_b, l - N_A, cos_q, sin_q, k2, vt, sinks_b[l - N_A], w_out, l, B)
        next_g = [pre_norm_g[l + 1]] if l < DEPTH - 1 else []
        if l == N_A - 1:
            next_g.append(kv_norm_g)
        h, ys = _mix_out(o_main, o_mem, w_out_l, h, post_norm_g[l], next_g)
        if ys:
            y = ys[0]
        if l == N_A - 1:
            k2, vt = _shared_kv(ys[1], w_kv_b, cos_k, sin_k)
    return h.reshape(B, S, D)
```
